```python
import math
import jax
import jax.numpy as jnp
from jax import lax
import numpy as np

D_MODEL = 1024
BATCH = 4
SEQ = 4096
DEPTH = 4
DEC_BATCH = 128
DEC_SEQ = 4
PAST_LEN = 2048
PAGE_SIZE = 128

N_A = DEPTH // 2
N_B = DEPTH - N_A
D_FF = ((8 * D_MODEL // 3 + 255) // 256) * 256
SSD_EXPAND = 2
D_INNER = SSD_EXPAND * D_MODEL
SSD_HEADDIM = 64
SSD_HEADS = D_INNER // SSD_HEADDIM
SSD_GROUPS = 8
SSD_RPG = SSD_HEADS // SSD_GROUPS
SSD_STATE = 128
CONV_W = 4
CONV_DIM = D_INNER + 2 * SSD_GROUPS * SSD_STATE
SSD_CHUNK = 128
DIL_CONFIGS = ((128, 1), (512, 4), (2048, 16))
N_DIL = 3
DIL_QH = 4
DIL_KVH = 2
DIL_QPK = DIL_QH // DIL_KVH
DIL_HD = 128
ALIBI_MAX = 8.0
N_MEM = 256
MEM_HEADS = 4
MEM_HD = 256
MEM_Q = MEM_HEADS * MEM_HD
EPS = 1e-6
SSD_IN = 2 * D_INNER + 2 * SSD_GROUPS * SSD_STATE + SSD_HEADS
IN_A = SSD_IN + MEM_Q
IN_B = N_DIL * DIL_QH * DIL_HD + MEM_Q
OUT_A = D_INNER + MEM_Q
OUT_B = DIL_QH * DIL_HD + MEM_Q
KV_SHARED = N_DIL * 2 * DIL_KVH * DIL_HD

kernel_name = 'yoco_ssd_dilated_macaron_decoder_step'


def rmsnorm(x, g):
    xf = x.astype(jnp.float32)
    y = xf * lax.rsqrt(jnp.mean(xf * xf, axis=-1, keepdims=True) + EPS)
    return (y * g.astype(jnp.float32)).astype(x.dtype)


def swiglu(h, w_gu, w_down):
    gu = h @ w_gu
    return (jax.nn.silu(gu[..., :D_FF]) * gu[..., D_FF:]) @ w_down


def alibi_slopes():
    n = N_DIL * DIL_QH
    i = jnp.arange(1, n + 1, dtype=jnp.float32)
    return jnp.exp2(-ALIBI_MAX * i / n).reshape(N_DIL, DIL_QH)


def causal_conv(xbc, conv_prev, w, b):
    L = xbc.shape[1]
    seq = jnp.concatenate([conv_prev.astype(xbc.dtype), xbc], axis=1)
    out = b + seq[:, 0:L] * w[0]
    for k in range(1, CONV_W):
        out = out + seq[:, k:k + L] * w[k]
    return jax.nn.silu(out), seq[:, L:]


def ssd_scan(x, dt, A, Bm, Cm, init, chunk):
    b, L = x.shape[:2]
    c = L // chunk
    xc = x.reshape(b, c, chunk, SSD_GROUPS, SSD_RPG, SSD_HEADDIM)
    dtc = dt.reshape(b, c, chunk, SSD_GROUPS, SSD_RPG)
    Bc = Bm.reshape(b, c, chunk, SSD_GROUPS, SSD_STATE)
    Cc = Cm.reshape(b, c, chunk, SSD_GROUPS, SSD_STATE)
    acum = jnp.cumsum(dtc * A, axis=2)
    xd = xc * dtc[..., None]
    at = jnp.moveaxis(acum, 2, -1)
    causal = jnp.tril(jnp.ones((chunk, chunk), dtype=bool))
    seg = at[..., :, None] - at[..., None, :]
    decay_ls = jnp.exp(jnp.where(causal, seg, -jnp.inf))
    cb = jnp.einsum('bclgn,bcsgn->bcgls', Cc, Bc)
    y_diag = jnp.einsum('bcgrls,bcsgrp->bclgrp', cb[:, :, :, None] * decay_ls, xd)
    decay_end = jnp.exp(acum[:, :, -1:] - acum)
    states = jnp.einsum('bclgn,bclgrp->bcgrpn', Bc, xd * decay_end[..., None]).astype(jnp.float32)
    chunk_decay = jnp.exp(acum[:, :, -1])

    def step(s, inp):
        dec, st = inp
        return s * dec[..., None, None] + st, s

    final, prev = lax.scan(step, init.astype(jnp.float32),
                           (jnp.moveaxis(chunk_decay, 1, 0), jnp.moveaxis(states, 1, 0)))
    prev = jnp.moveaxis(prev, 0, 1)
    y_off = jnp.einsum('bclgn,bcgrpn->bclgrp', Cc, prev) * jnp.exp(acum)[..., None]
    y = (y_diag + y_off).reshape(b, L, SSD_GROUPS, SSD_RPG, SSD_HEADDIM)
    return y, final


def mamba2_mix(u, conv_prev, ssm_prev, conv_w, conv_b, dt_bias, a_log, d_skip, gnorm_g, chunk):
    b, L = u.shape[:2]
    z = u[..., :D_INNER]
    xbc, conv_new = causal_conv(u[..., D_INNER:D_INNER + CONV_DIM], conv_prev, conv_w, conv_b)
    dt_raw = u[..., D_INNER + CONV_DIM:]
    xs = xbc[..., :D_INNER].reshape(b, L, SSD_GROUPS, SSD_RPG, SSD_HEADDIM)
    Bm = xbc[..., D_INNER:D_INNER + SSD_GROUPS * SSD_STATE].reshape(b, L, SSD_GROUPS, SSD_STATE)
    Cm = xbc[..., D_INNER + SSD_GROUPS * SSD_STATE:].reshape(b, L, SSD_GROUPS, SSD_STATE)
    dt = jax.nn.softplus((dt_raw + dt_bias).astype(jnp.float32)).reshape(b, L, SSD_GROUPS, SSD_RPG)
    A = -jnp.exp(a_log.astype(jnp.float32)).reshape(SSD_GROUPS, SSD_RPG)
    init = ssm_prev.reshape(b, SSD_GROUPS, SSD_RPG, SSD_HEADDIM, SSD_STATE)
    y, final = ssd_scan(xs, dt, A, Bm, Cm, init, chunk)
    y = y + d_skip.reshape(SSD_GROUPS, SSD_RPG)[..., None] * xs
    y = y.reshape(b, L, D_INNER).astype(u.dtype)
    y = rmsnorm(y * jax.nn.silu(z), gnorm_g)
    return y, conv_new, final.reshape(b, SSD_HEADS, SSD_HEADDIM, SSD_STATE).astype(ssm_prev.dtype)


def mem_kv_from(mem, g, w):
    kv = rmsnorm(mem, g) @ w
    return kv.reshape(mem.shape[0], mem.shape[1], 2, MEM_HEADS, MEM_HD)


def mem_attention(q, mem_kv):
    b, L = q.shape[:2]
    qh = q.reshape(b, L, MEM_HEADS, MEM_HD)
    s = jnp.einsum('blhd,bmhd->bhlm', qh, mem_kv[:, :, 0]).astype(jnp.float32) * (MEM_HD ** -0.5)
    p = jax.nn.softmax(s, axis=-1)
    o = jnp.einsum('bhlm,bmhd->blhd', p, mem_kv[:, :, 1].astype(jnp.float32))
    return o.reshape(b, L, MEM_Q).astype(q.dtype)


def dilated_prompt(q, kv, slopes, window, dil):
    b, S = q.shape[:2]
    nk = window // dil
    span = dil * nk
    Lp = -(-S // span) * span
    q = jnp.pad(q, ((0, 0), (0, Lp - S), (0, 0), (0, 0)))
    kv = jnp.pad(kv, ((0, 0), (0, Lp - S), (0, 0), (0, 0), (0, 0)))
    n = Lp // dil
    nb = n // nk

    def split_res(t):
        t = t.reshape((b, n, dil) + t.shape[2:])
        t = jnp.moveaxis(t, 2, 1)
        return t.reshape((b, dil, nb, nk) + t.shape[3:])

    def merge_res(t):
        rest = t.shape[4:]
        t = jnp.moveaxis(t.reshape((b, dil, n) + rest), 1, 2)
        return t.reshape((b, Lp) + rest)[:, :S]

    qr = split_res(q).reshape(b, dil, nb, nk, DIL_KVH, DIL_QPK, DIL_HD)
    kvr = split_res(kv)
    kv_prev = jnp.pad(kvr[:, :, :-1], ((0, 0), (0, 0), (1, 0), (0, 0), (0, 0), (0, 0), (0, 0)))
    kvb = jnp.concatenate([kv_prev, kvr], axis=3)
    s = jnp.einsum('bdnqkgh,bdnskh->bdnkgqs', qr, kvb[:, :, :, :, 0]).astype(jnp.float32) * (DIL_HD ** -0.5)
    du = nk + jnp.arange(nk)[:, None] - jnp.arange(2 * nk)[None, :]
    first = (jnp.arange(nb) == 0)[:, None, None] & (jnp.arange(2 * nk) < nk)[None, None, :]
    valid = (du >= 0)[None] & (du <= nk)[None] & ~first
    bias = -slopes.reshape(DIL_KVH, DIL_QPK)[:, :, None, None] * (dil * du).astype(jnp.float32)
    s = jnp.where(valid[:, None, None], s + bias, -jnp.inf)
    lse = jax.nn.logsumexp(s, axis=-1)
    p = jnp.exp(s - lse[..., None])
    o = jnp.einsum('bdnkgqs,bdnskh->bdnqkgh', p, kvb[:, :, :, :, 1].astype(jnp.float32))
    o = merge_res(o.reshape(b, dil, nb, nk, DIL_QH, DIL_HD))
    lse = merge_res(jnp.moveaxis(lse, -1, 3).reshape(b, dil, nb, nk, DIL_QH))
    return o, lse


def dilated_sample(q, kv_ext, slopes, window, dil):
    b, T = q.shape[:2]
    Lb = kv_ext.shape[1] - T
    nk = window // dil
    j = jnp.arange(nk + 1)
    idx = Lb + jnp.arange(T)[:, None] - dil * j[None, :]
    valid = idx >= 0
    kvg = kv_ext[:, jnp.maximum(idx, 0)]
    qg = q.reshape(b, T, DIL_KVH, DIL_QPK, DIL_HD)
    s = jnp.einsum('btkgh,btjkh->btkgj', qg, kvg[:, :, :, 0]).astype(jnp.float32) * (DIL_HD ** -0.5)
    bias = -slopes.reshape(DIL_KVH, DIL_QPK)[:, :, None] * (dil * j).astype(jnp.float32)
    s = jnp.where(valid[:, None, None, :], s + bias, -jnp.inf)
    lse = jax.nn.logsumexp(s, axis=-1)
    p = jnp.exp(s - lse[..., None])
    o = jnp.einsum('btkgj,btjkh->btkgh', p, kvg[:, :, :, 1].astype(jnp.float32))
    return o.reshape(b, T, DIL_QH, DIL_HD), lse.reshape(b, T, DIL_QH)


def dilated_mix(q, kv_shared, win_bufs, slopes):
    b, L = q.shape[:2]
    outs, lses = [], []
    for gi in range(N_DIL):
        window, dil = DIL_CONFIGS[gi]
        kv_g = kv_shared[:, :, gi]
        if win_bufs is None:
            o, lse = dilated_prompt(q[:, :, gi], kv_g, slopes[gi], window, dil)
        else:
            kv_ext = jnp.concatenate([win_bufs[gi].astype(kv_g.dtype), kv_g], axis=1)
            o, lse = dilated_sample(q[:, :, gi], kv_ext, slopes[gi], window, dil)
        outs.append(o)
        lses.append(lse)
    o = jnp.stack(outs, axis=2)
    alpha = jax.nn.softmax(jnp.stack(lses, axis=2), axis=2)
    y = jnp.sum(alpha[..., None] * o, axis=2)
    return y.reshape(b, L, DIL_QH * DIL_HD).astype(q.dtype)


def trunk(x, mem_kv, conv_prev, ssm_prev, win_bufs, ssd_chunk, norm_g, w_ffn_gu, w_ffn_down,
          w_in_a, conv_w, conv_b, dt_bias, a_log, d_skip, ssd_norm_g, w_out_a, w_in_b, w_out_b,
          kv_norm_g, w_kv_shared, final_norm_g):
    b, L = x.shape[:2]
    slopes = alibi_slopes()
    conv_new, ssm_new = [], []
    kv_shared = None
    for l in range(DEPTH):
        g = norm_g[l]
        x = x + 0.5 * rmsnorm(swiglu(rmsnorm(x, g[0]), w_ffn_gu[l, 0], w_ffn_down[l, 0]), g[1])
        h = rmsnorm(x, g[2])
        if l < N_A:
            u = h @ w_in_a[l]
            y_mix, cn, sn = mamba2_mix(u[..., :SSD_IN], conv_prev[l], ssm_prev[l], conv_w[l], conv_b[l],
                                       dt_bias[l], a_log[l], d_skip[l], ssd_norm_g[l], ssd_chunk)
            conv_new.append(cn)
            ssm_new.append(sn)
            q_mem = u[..., SSD_IN:]
            w_out = w_out_a[l]
        else:
            i = l - N_A
            u = h @ w_in_b[i]
            q_dil = u[..., :N_DIL * DIL_QH * DIL_HD].reshape(b, L, N_DIL, DIL_QH, DIL_HD)
            y_mix = dilated_mix(q_dil, kv_shared, win_bufs, slopes)
            q_mem = u[..., N_DIL * DIL_QH * DIL_HD:]
            w_out = w_out_b[i]
        y = jnp.concatenate([y_mix, mem_attention(q_mem, mem_kv[l])], axis=-1) @ w_out
        x = x + rmsnorm(y, g[3])
        x = x + 0.5 * rmsnorm(swiglu(rmsnorm(x, g[4]), w_ffn_gu[l, 1], w_ffn_down[l, 1]), g[5])
        if l == N_A - 1:
            kv_shared = (rmsnorm(x, kv_norm_g) @ w_kv_shared).reshape(b, L, N_DIL, 2, DIL_KVH, DIL_HD)
    return rmsnorm(x, final_norm_g), jnp.stack(conv_new), jnp.stack(ssm_new), kv_shared


def setup_inputs(seed: int = 0) -> dict:
    key = jax.random.key(seed)
    ks = jax.random.split(key, 32)
    f32 = jnp.float32

    def nrm(k, shape, scale):
        return jax.random.normal(k, shape, f32) * scale

    def gain(k, shape):
        return 1.0 + 0.05 * jax.random.normal(k, shape, f32)

    wl = [min(w, PAST_LEN) for w, _ in DIL_CONFIGS]
    dt0 = jnp.exp(jax.random.uniform(ks[10], (N_A, SSD_HEADS), f32, math.log(1e-3), math.log(1e-1)))
    return {
        'x_prompt': nrm(ks[0], (BATCH, SEQ, D_MODEL), 1.0),
        'x_sample': nrm(ks[1], (DEC_BATCH, DEC_SEQ, D_MODEL), 1.0),
        'mem_prompt': nrm(ks[2], (BATCH, N_MEM, D_MODEL), 1.0),
        'cache_mem_kv': nrm(ks[3], (DEPTH, DEC_BATCH, N_MEM, 2, MEM_HEADS, MEM_HD), 1.0),
        'state_ssm': nrm(ks[4], (N_A, DEC_BATCH, SSD_HEADS, SSD_HEADDIM, SSD_STATE), 0.1),
        'state_conv': nrm(ks[5], (N_A, DEC_BATCH, CONV_W - 1, CONV_DIM), 1.0),
        'cache_win_g1': nrm(ks[6], (DEC_BATCH, wl[0], 2, DIL_KVH, DIL_HD), 1.0),
        'cache_win_g2': nrm(ks[7], (DEC_BATCH, wl[1], 2, DIL_KVH, DIL_HD), 1.0),
        'cache_win_g3': nrm(ks[8], (DEC_BATCH, wl[2], 2, DIL_KVH, DIL_HD), 1.0),
        'norm_g': gain(ks[9], (DEPTH, 6, D_MODEL)),
        'w_ffn_gu': nrm(ks[11], (DEPTH, 2, D_MODEL, 2 * D_FF), D_MODEL ** -0.5),
        'w_ffn_down': nrm(ks[12], (DEPTH, 2, D_FF, D_MODEL), D_FF ** -0.5),
        'w_in_a': nrm(ks[13], (N_A, D_MODEL, IN_A), D_MODEL ** -0.5),
        'conv_w': nrm(ks[14], (N_A, CONV_W, CONV_DIM), CONV_W ** -0.5),
        'conv_b': nrm(ks[15], (N_A, CONV_DIM), 0.01),
        'dt_bias': dt0 + jnp.log(-jnp.expm1(-dt0)),
        'a_log': jnp.log(jax.random.uniform(ks[16], (N_A, SSD_HEADS), f32, 1.0, 16.0)),
        'd_skip': gain(ks[17], (N_A, SSD_HEADS)),
        'ssd_norm_g': gain(ks[18], (N_A, D_INNER)),
        'w_out_a': nrm(ks[19], (N_A, OUT_A, D_MODEL), OUT_A ** -0.5),
        'w_in_b': nrm(ks[20], (N_B, D_MODEL, IN_B), D_MODEL ** -0.5),
        'w_out_b': nrm(ks[21], (N_B, OUT_B, D_MODEL), OUT_B ** -0.5),
        'kv_norm_g': gain(ks[22], (D_MODEL,)),
        'w_kv_shared': nrm(ks[23], (D_MODEL, KV_SHARED), D_MODEL ** -0.5),
        'mem_norm_g': gain(ks[24], (DEPTH, D_MODEL)),
        'w_mem_kv': nrm(ks[25], (DEPTH, D_MODEL, 2 * MEM_Q), D_MODEL ** -0.5),
        'final_norm_g': gain(ks[26], (D_MODEL,)),
    }


def reference(x_prompt, x_sample, mem_prompt, cache_mem_kv, state_ssm, state_conv, cache_win_g1,
              cache_win_g2, cache_win_g3, norm_g, w_ffn_gu, w_ffn_down, w_in_a, conv_w, conv_b,
              dt_bias, a_log, d_skip, ssd_norm_g, w_out_a, w_in_b, w_out_b, kv_norm_g, w_kv_shared,
              mem_norm_g, w_mem_kv, final_norm_g):
    bp, S = x_prompt.shape[:2]
    mem_kv_p = jnp.stack([mem_kv_from(mem_prompt, mem_norm_g[l], w_mem_kv[l]) for l in range(DEPTH)])
    conv0 = jnp.zeros((N_A, bp, CONV_W - 1, CONV_DIM), x_prompt.dtype)
    ssm0 = jnp.zeros((N_A, bp, SSD_HEADS, SSD_HEADDIM, SSD_STATE), x_prompt.dtype)
    y_prompt, conv_p, ssm_p, kv_p = trunk(
        x_prompt, mem_kv_p, conv0, ssm0, None, min(SSD_CHUNK, S), norm_g, w_ffn_gu, w_ffn_down,
        w_in_a, conv_w, conv_b, dt_bias, a_log, d_skip, ssd_norm_g, w_out_a, w_in_b, w_out_b,
        kv_norm_g, w_kv_shared, final_norm_g)
    win_p = [kv_p[:, S - min(w, S):, gi] for gi, (w, _) in enumerate(DIL_CONFIGS)]
    y_sample, conv_s, ssm_s, kv_s = trunk(
        x_sample, cache_mem_kv, state_conv, state_ssm, (cache_win_g1, cache_win_g2, cache_win_g3),
        x_sample.shape[1], norm_g, w_ffn_gu, w_ffn_down, w_in_a, conv_w, conv_b, dt_bias, a_log,
        d_skip, ssd_norm_g, w_out_a, w_in_b, w_out_b, kv_norm_g, w_kv_shared, final_norm_g)
    return (y_prompt, y_sample, mem_kv_p, ssm_p, conv_p, win_p[0], win_p[1], win_p[2],
            ssm_s, conv_s, kv_s[:, :, 0], kv_s[:, :, 1], kv_s[:, :, 2])
```

```python
import functools

import jax
import jax.numpy as jnp
from jax import lax
from jax.experimental import pallas as pl
from jax.experimental.pallas import tpu as pltpu

F32 = jnp.float32
BF16 = jnp.bfloat16
HIGHEST = lax.Precision.HIGHEST

D_MODEL = 1024
DEPTH = 4
N_A = 2
D_FF = 2816
D_INNER = 2048
SSD_HEADDIM = 64
SSD_HEADS = 32
SSD_GROUPS = 8
SSD_STATE = 128
CONV_W = 4
CONV_DIM = D_INNER + 2 * SSD_GROUPS * SSD_STATE
BC_DIM = SSD_GROUPS * SSD_STATE
DIL_CONFIGS = ((128, 1), (512, 4), (2048, 16))
N_DIL = 3
DIL_QH = 4
DIL_KVH = 2
DIL_HD = 128
DIL_NK = 128
DIL_GW = DIL_QH * DIL_HD
N_MEM = 256
MEM_HEADS = 4
MEM_HD = 256
MEM_Q = MEM_HEADS * MEM_HD
EPS = 1e-6
SSD_MAIN = 2 * D_INNER + 2 * BC_DIM
SSD_IN = SSD_MAIN + SSD_HEADS
U_A = SSD_MAIN + MEM_Q
U_B = MEM_Q + N_DIL * DIL_GW
KV_W = N_DIL * 2 * DIL_KVH * DIL_HD

LANES = 128
CHUNK = 128
FF_CHUNK = 256
VMEM_LIMIT = 56 * 2 ** 20


def _params(*sem):
    return pltpu.CompilerParams(dimension_semantics=sem, vmem_limit_bytes=VMEM_LIMIT)


def _const_spec(shape):
    return pl.BlockSpec(shape, lambda *_: (0,) * len(shape), pipeline_mode=pl.Buffered(1))


def _rms(x, g):
    return x * lax.rsqrt(jnp.mean(x * x, axis=-1, keepdims=True) + EPS) * g


def _silu(x):
    return x / (1.0 + jnp.exp(-x))


def _dot(a, b):
    return jnp.dot(a, b, preferred_element_type=F32)


def _dot_nt(a, b):
    return lax.dot_general(a, b, (((1,), (1,)), ((), ())), preferred_element_type=F32)


def _row_tile(t, pref):
    return pref if t % pref == 0 else t


def _ffn_body(x_ref, g_ref, wgu_ref, wd_ref, o_ref, acc_ref, *, final):
    x = x_ref[...]
    h = _rms(x, g_ref[0:1, :]).astype(BF16)
    for c in range(D_FF // FF_CHUNK):
        lo = c * FF_CHUNK
        gate = _dot(h, wgu_ref[:, lo:lo + FF_CHUNK])
        up = _dot(h, wgu_ref[:, D_FF + lo:D_FF + lo + FF_CHUNK])
        part = _dot((_silu(gate) * up).astype(BF16), wd_ref[lo:lo + FF_CHUNK, :])
        if c == 0:
            acc_ref[...] = part
        else:
            acc_ref[...] += part
    y = x + 0.5 * _rms(acc_ref[...], g_ref[1:2, :])
    if final:
        y = _rms(y, g_ref[2:3, :])
    o_ref[...] = y


def _ffn(x, g3, wgu, wd, final=False):
    t = x.shape[0]
    tm = _row_tile(t, 512)
    return pl.pallas_call(
        functools.partial(_ffn_body, final=final),
        grid=(t // tm,),
        in_specs=[pl.BlockSpec((tm, D_MODEL), lambda i: (i, 0)),
                  _const_spec((3, D_MODEL)),
                  _const_spec((D_MODEL, 2 * D_FF)),
                  _const_spec((D_FF, D_MODEL))],
        out_specs=pl.BlockSpec((tm, D_MODEL), lambda i: (i, 0)),
        out_shape=jax.ShapeDtypeStruct((t, D_MODEL), F32),
        scratch_shapes=[pltpu.VMEM((tm, D_MODEL), F32)],
        compiler_params=_params("parallel"),
        name="ffn",
    )(x, g3, wgu, wd)


def _nmm_body(x_ref, g_ref, w_ref, o_ref, h_ref):
    @pl.when(pl.program_id(1) == 0)
    def _():
        h_ref[...] = _rms(x_ref[...], g_ref[...]).astype(BF16)

    o_ref[...] = _dot(h_ref[...], w_ref[...])


def _norm_matmul(x, g, w):
    t, n = x.shape[0], w.shape[1]
    tm = _row_tile(t, 1024)
    tn = 512 if n % 512 == 0 else n
    return pl.pallas_call(
        _nmm_body,
        grid=(t // tm, n // tn),
        in_specs=[pl.BlockSpec((tm, D_MODEL), lambda i, j: (i, 0)),
                  pl.BlockSpec((1, D_MODEL), lambda i, j: (0, 0)),
                  pl.BlockSpec((D_MODEL, tn), lambda i, j: (0, j))],
        out_specs=pl.BlockSpec((tm, tn), lambda i, j: (i, j)),
        out_shape=jax.ShapeDtypeStruct((t, n), F32),
        scratch_shapes=[pltpu.VMEM((tm, D_MODEL), BF16)],
        compiler_params=_params("parallel", "arbitrary"),
        name="norm_matmul",
    )(x, g.reshape(1, D_MODEL), w)


def _out_a_body(y_ref, m_ref, x_ref, g_ref, w_ref, o_ref):
    acc = _dot(y_ref[...].astype(BF16), w_ref[0:D_INNER, :])
    acc += _dot(m_ref[...].astype(BF16), w_ref[D_INNER:D_INNER + MEM_Q, :])
    o_ref[...] = x_ref[...] + _rms(acc, g_ref[...])


def _out_proj_a(y, m, x, g, w):
    t = x.shape[0]
    tm = _row_tile(t, 512)
    row = lambda width: pl.BlockSpec((tm, width), lambda i: (i, 0))
    return pl.pallas_call(
        _out_a_body,
        grid=(t // tm,),
        in_specs=[row(D_INNER), row(MEM_Q), row(D_MODEL), _const_spec((1, D_MODEL)),
                  _const_spec((D_INNER + MEM_Q, D_MODEL))],
        out_specs=row(D_MODEL),
        out_shape=jax.ShapeDtypeStruct((t, D_MODEL), F32),
        compiler_params=_params("parallel"),
        name="out_proj_a",
    )(y, m, x, g.reshape(1, D_MODEL), w)


def _out_b_body(o1_ref, o2_ref, o3_ref, l1_ref, l2_ref, l3_ref, m_ref, x_ref, g_ref, w_ref, o_ref):
    l1, l2, l3 = l1_ref[...], l2_ref[...], l3_ref[...]
    mx = jnp.maximum(jnp.maximum(l1, l2), l3)
    e1, e2, e3 = jnp.exp(l1 - mx), jnp.exp(l2 - mx), jnp.exp(l3 - mx)
    den = e1 + e2 + e3
    mix = (e1 / den) * o1_ref[...] + (e2 / den) * o2_ref[...] + (e3 / den) * o3_ref[...]
    acc = _dot(mix.astype(BF16), w_ref[0:DIL_GW, :])
    acc += _dot(m_ref[...].astype(BF16), w_ref[DIL_GW:DIL_GW + MEM_Q, :])
    o_ref[...] = x_ref[...] + _rms(acc, g_ref[...])


def _out_proj_b(outs, lses, m, x, g, w):
    t = x.shape[0]
    tm = _row_tile(t, 512)
    row = lambda width: pl.BlockSpec((tm, width), lambda i: (i, 0))
    return pl.pallas_call(
        _out_b_body,
        grid=(t // tm,),
        in_specs=[row(DIL_GW)] * 6 + [row(MEM_Q), row(D_MODEL), _const_spec((1, D_MODEL)),
                                      _const_spec((DIL_GW + MEM_Q, D_MODEL))],
        out_specs=row(D_MODEL),
        out_shape=jax.ShapeDtypeStruct((t, D_MODEL), F32),
        compiler_params=_params("parallel"),
        name="out_proj_b",
    )(*outs, *lses, m, x, g.reshape(1, D_MODEL), w)


def _softmax_rows(s):
    e = jnp.exp(s - jnp.max(s, axis=-1, keepdims=True))
    return e / jnp.sum(e, axis=-1, keepdims=True)


def _mem_prompt_body(q_ref, kv_ref, o_ref):
    for h in range(MEM_HEADS):
        lo = h * MEM_HD
        q = q_ref[:, lo:lo + MEM_HD].astype(BF16)
        k = kv_ref[:, lo:lo + MEM_HD].astype(BF16)
        v = kv_ref[:, MEM_Q + lo:MEM_Q + lo + MEM_HD].astype(BF16)
        p = _softmax_rows(_dot_nt(q, k) * (MEM_HD ** -0.5))
        o_ref[:, lo:lo + MEM_HD] = _dot(p.astype(BF16), v)


def _mem_attn_prompt(u, q_block, mem_kv, b, s):
    c = u.shape[1]
    tq = _row_tile(s, 512)
    out = pl.pallas_call(
        _mem_prompt_body,
        grid=(b, s // tq),
        in_specs=[pl.BlockSpec((None, tq, MEM_Q), lambda i, j: (i, j, q_block)),
                  pl.BlockSpec((None, N_MEM, 2 * MEM_Q), lambda i, j: (i, 0, 0))],
        out_specs=pl.BlockSpec((None, tq, MEM_Q), lambda i, j: (i, j, 0)),
        out_shape=jax.ShapeDtypeStruct((b, s, MEM_Q), F32),
        compiler_params=_params("parallel", "parallel"),
        name="mem_attn_prompt",
    )(u.reshape(b, s, c), mem_kv)
    return out.reshape(b * s, MEM_Q)


MEM_BT = 4


def _mem_sample_body(q_ref, kv_ref, o_ref, *, tlen):
    rows = MEM_BT * tlen
    owner = lax.broadcasted_iota(jnp.int32, (rows, 1), 0) // tlen
    q_all = q_ref[...]
    for h in range(MEM_HEADS):
        lo = h * MEM_HD
        q = q_all[:, lo:lo + MEM_HD].astype(BF16)
        s = jnp.zeros((rows, N_MEM), F32)
        for bb in range(MEM_BT):
            k = kv_ref[bb, :, lo:lo + MEM_HD].astype(BF16)
            s = jnp.where(owner == bb, _dot_nt(q, k), s)
        p = _softmax_rows(s * (MEM_HD ** -0.5)).astype(BF16)
        o = jnp.zeros((rows, MEM_HD), F32)
        for bb in range(MEM_BT):
            v = kv_ref[bb, :, MEM_Q + lo:MEM_Q + lo + MEM_HD].astype(BF16)
            o = jnp.where(owner == bb, _dot(p, v), o)
        o_ref[:, lo:lo + MEM_HD] = o


def _mem_attn_sample(u, q_block, mem_kv, b, tlen):
    rows = MEM_BT * tlen
    return pl.pallas_call(
        functools.partial(_mem_sample_body, tlen=tlen),
        grid=(b // MEM_BT,),
        in_specs=[pl.BlockSpec((rows, MEM_Q), lambda i: (i, q_block)),
                  pl.BlockSpec((MEM_BT, N_MEM, 2 * MEM_Q), lambda i: (i, 0, 0))],
        out_specs=pl.BlockSpec((rows, MEM_Q), lambda i: (i, 0)),
        out_shape=jax.ShapeDtypeStruct((b * tlen, MEM_Q), F32),
        compiler_params=_params("parallel"),
        name="mem_attn_sample",
    )(u, mem_kv)


def _dil_prompt_body(sl_ref, q_ref, kp_ref, kc_ref, o_ref, l_ref, *, gi, dil):
    has_prev = pl.program_id(2) > 0
    qi = lax.broadcasted_iota(jnp.int32, (DIL_NK, DIL_NK), 0)
    kj = lax.broadcasted_iota(jnp.int32, (DIL_NK, DIL_NK), 1)
    du_prev = DIL_NK + qi - kj
    du_own = qi - kj
    valid_prev = (du_prev <= DIL_NK) & has_prev
    valid_own = du_own >= 0
    dist_prev = (dil * du_prev).astype(F32)
    dist_own = (dil * du_own).astype(F32)
    scale = DIL_HD ** -0.5
    for kvh in range(DIL_KVH):
        klo = kvh * DIL_HD
        vlo = DIL_KVH * DIL_HD + klo
        k_prev = kp_ref[:, klo:klo + DIL_HD].astype(BF16)
        k_own = kc_ref[:, klo:klo + DIL_HD].astype(BF16)
        v_prev = kp_ref[:, vlo:vlo + DIL_HD].astype(BF16)
        v_own = kc_ref[:, vlo:vlo + DIL_HD].astype(BF16)
        for qh in range(DIL_QH // DIL_KVH):
            head = kvh * (DIL_QH // DIL_KVH) + qh
            lo = head * DIL_HD
            neg_slope = -sl_ref[gi * DIL_QH + head]
            q = q_ref[:, lo:lo + DIL_HD].astype(BF16)
            s_prev = jnp.where(valid_prev, _dot_nt(q, k_prev) * scale + neg_slope * dist_prev, -jnp.inf)
            s_own = jnp.where(valid_own, _dot_nt(q, k_own) * scale + neg_slope * dist_own, -jnp.inf)
            mx = jnp.maximum(jnp.max(s_prev, axis=-1, keepdims=True), jnp.max(s_own, axis=-1, keepdims=True))
            e_prev = jnp.exp(s_prev - mx)
            e_own = jnp.exp(s_own - mx)
            den = jnp.sum(e_prev, axis=-1, keepdims=True) + jnp.sum(e_own, axis=-1, keepdims=True)
            o = _dot((e_prev / den).astype(BF16), v_prev) + _dot((e_own / den).astype(BF16), v_own)
            o_ref[:, lo:lo + DIL_HD] = o
            l_ref[:, lo:lo + DIL_HD] = jnp.broadcast_to(mx + jnp.log(den), (DIL_NK, DIL_HD))


def _dil_attn_prompt(slopes, u, kv, gi, b, s):
    window, dil = DIL_CONFIGS[gi]
    assert window // dil == DIL_NK and s % (dil * DIL_NK) == 0
    n = s // dil
    qb = U_B // DIL_GW
    kb = KV_W // DIL_GW
    blk = lambda imap: pl.BlockSpec((None, DIL_NK, DIL_GW), imap)
    o, lse = pl.pallas_call(
        functools.partial(_dil_prompt_body, gi=gi, dil=dil),
        grid=(b, dil, n // DIL_NK),
        in_specs=[pl.BlockSpec(memory_space=pltpu.SMEM),
                  blk(lambda i, r, j: (i, j, r * qb + MEM_Q // DIL_GW + gi)),
                  blk(lambda i, r, j: (i, jnp.maximum(j - 1, 0), r * kb + gi)),
                  blk(lambda i, r, j: (i, j, r * kb + gi))],
        out_specs=[blk(lambda i, r, j: (i, j, r)), blk(lambda i, r, j: (i, j, r))],
        out_shape=[jax.ShapeDtypeStruct((b, n, dil * DIL_GW), F32)] * 2,
        compiler_params=_params("parallel", "parallel", "arbitrary"),
        name=f"dil_attn_prompt_g{gi}",
    )(slopes, u.reshape(b, n, dil * U_B), kv.reshape(b, n, dil * KV_W), kv.reshape(b, n, dil * KV_W))
    return o.reshape(b * s, DIL_GW), lse.reshape(b * s, DIL_GW)


DIL_BT = 4


def _dil_sample_body(sl_ref, q_ref, buf_ref, new_ref, o_ref, l_ref, *, gi, dil, tlen):
    mi = lax.broadcasted_iota(jnp.int32, (DIL_NK, 1), 0)
    si = lax.broadcasted_iota(jnp.int32, (tlen, 1), 0)
    scale = DIL_HD ** -0.5

    def per_batch(bb, carry):
        q_all = q_ref[bb]
        new = new_ref[bb]
        for t in range(tlen):
            if dil == 1:
                col0 = 0
                j_buf, ok_buf = DIL_NK + t - mi, mi >= t
                j_new, ok_new = t - si, si <= t
            else:
                col0 = t * DIL_GW
                j_buf, ok_buf = DIL_NK - mi, mi >= 0
                j_new, ok_new = jnp.zeros_like(si), si == t
            dist_buf = (dil * j_buf).astype(F32)
            dist_new = (dil * j_new).astype(F32)
            for kvh in range(DIL_KVH):
                klo = kvh * DIL_HD
                vlo = DIL_KVH * DIL_HD + klo
                k_buf = buf_ref[bb, :, col0 + klo:col0 + klo + DIL_HD]
                v_buf = buf_ref[bb, :, col0 + vlo:col0 + vlo + DIL_HD]
                k_new = new[:, klo:klo + DIL_HD]
                v_new = new[:, vlo:vlo + DIL_HD]
                for qh in range(DIL_QH // DIL_KVH):
                    head = kvh * (DIL_QH // DIL_KVH) + qh
                    lo = head * DIL_HD
                    neg_slope = -sl_ref[gi * DIL_QH + head]
                    q = q_all[t:t + 1, lo:lo + DIL_HD]
                    s_buf = jnp.sum(q * k_buf, axis=-1, keepdims=True) * scale + neg_slope * dist_buf
                    s_new = jnp.sum(q * k_new, axis=-1, keepdims=True) * scale + neg_slope * dist_new
                    s_buf = jnp.where(ok_buf, s_buf, -jnp.inf)
                    s_new = jnp.where(ok_new, s_new, -jnp.inf)
                    mx = jnp.maximum(jnp.max(s_buf, axis=0, keepdims=True), jnp.max(s_new, axis=0, keepdims=True))
                    e_buf = jnp.exp(s_buf - mx)
                    e_new = jnp.exp(s_new - mx)
                    den = jnp.sum(e_buf, axis=0, keepdims=True) + jnp.sum(e_new, axis=0, keepdims=True)
                    acc = jnp.sum(e_buf * v_buf, axis=0, keepdims=True) + jnp.sum(e_new * v_new, axis=0, keepdims=True)
                    o_ref[bb, t:t + 1, lo:lo + DIL_HD] = acc / den
                    l_ref[bb, t:t + 1, lo:lo + DIL_HD] = jnp.broadcast_to(mx + jnp.log(den), (1, DIL_HD))
        return carry

    lax.fori_loop(0, DIL_BT, per_batch, 0)


def _dil_attn_sample(slopes, u, kv_new, win_buf, gi, b, tlen):
    window, dil = DIL_CONFIGS[gi]
    lb = win_buf.shape[1]
    assert window // dil == DIL_NK and lb == window and (dil == 1 or tlen <= dil)
    row_w = dil * DIL_GW
    fetch_w = DIL_GW if dil == 1 else tlen * DIL_GW
    assert row_w % fetch_w == 0
    o, lse = pl.pallas_call(
        functools.partial(_dil_sample_body, gi=gi, dil=dil, tlen=tlen),
        grid=(b // DIL_BT,),
        in_specs=[pl.BlockSpec(memory_space=pltpu.SMEM),
                  pl.BlockSpec((DIL_BT, tlen, DIL_GW), lambda i: (i, 0, MEM_Q // DIL_GW + gi)),
                  pl.BlockSpec((DIL_BT, DIL_NK, fetch_w), lambda i: (i, 0, 0)),
                  pl.BlockSpec((DIL_BT, tlen, DIL_GW), lambda i: (i, 0, gi))],
        out_specs=[pl.BlockSpec((DIL_BT, tlen, DIL_GW), lambda i: (i, 0, 0))] * 2,
        out_shape=[jax.ShapeDtypeStruct((b, tlen, DIL_GW), F32)] * 2,
        compiler_params=_params("parallel"),
        name=f"dil_attn_sample_g{gi}",
    )(slopes, u.reshape(b, tlen, U_B), win_buf.reshape(b, DIL_NK, row_w), kv_new.reshape(b, tlen, KV_W))
    return o.reshape(b * tlen, DIL_GW), lse.reshape(b * tlen, DIL_GW)


def _pair_cols(a, pair, lane):
    h0 = 2 * pair
    return jnp.where(lane < SSD_HEADDIM, a[:, h0:h0 + 1], a[:, h0 + 1:h0 + 2])


def _ssd_intra(xs_ref, b_ref, c_ref, dt_raw, par_ref, tri_ref, seg_ref, y_ref, xddt_ref):
    dt_bias, a_log = par_ref[0:1, 0:LANES], par_ref[1:2, 0:LANES]
    pre = dt_raw + dt_bias
    dt = jnp.maximum(pre, 0.0) + jnp.log(1.0 + jnp.exp(-jnp.abs(pre)))
    d_a = dt * -jnp.exp(a_log)
    tri = tri_ref[...]
    acum = jnp.dot(tri, d_a, precision=HIGHEST, preferred_element_type=F32)
    a_end = jnp.dot(seg_ref[...], d_a, precision=HIGHEST, preferred_element_type=F32)
    acum_t = acum.T
    causal = tri > 0.5
    e_acum = jnp.exp(acum)
    e_end = jnp.exp(a_end)
    d_end = jnp.exp(a_end - acum)
    lane = lax.broadcasted_iota(jnp.int32, (CHUNK, LANES), 1)
    for grp in range(SSD_GROUPS):
        glo = grp * SSD_STATE
        cb = _dot_nt(c_ref[:, glo:glo + SSD_STATE].astype(BF16), b_ref[:, glo:glo + SSD_STATE].astype(BF16))
        for half in range(2):
            pair = 2 * grp + half
            plo = pair * LANES
            xs = xs_ref[:, plo:plo + LANES]
            xd = xs * _pair_cols(dt, pair, lane)
            xd16 = xd.astype(BF16)
            yd = []
            for k in range(2):
                h = 2 * pair + k
                decay = jnp.exp(jnp.where(causal, acum[:, h:h + 1] - acum_t[h:h + 1, :], -jnp.inf))
                yd.append(_dot((cb * decay).astype(BF16), xd16))
            y_ref[:, plo:plo + LANES] = jnp.where(lane < SSD_HEADDIM, yd[0], yd[1]) + par_ref[2:3, plo:plo + LANES] * xs
            xddt_ref[plo:plo + LANES, :] = (xd * _pair_cols(d_end, pair, lane)).T
    return e_acum, e_end


def _ssd_finish(y_ref, z_ref, par_ref, o_ref):
    o_ref[...] = _rms(y_ref[...] * _silu(z_ref[...]), par_ref[3:4, :])


def _ssd_prompt_body(z_ref, x_ref, b_ref, c_ref, dt_ref, cw_ref, par_ref, tri_ref, seg_ref,
                     o_ref, st_ref, ext_ref, xbc_ref, y_ref, xddt_ref):
    first = pl.program_id(1) == 0

    @pl.when(first)
    def _():
        ext_ref[0:8, :] = jnp.zeros((8, CONV_DIM), F32)
        st_ref[...] = jnp.zeros_like(st_ref)

    ext_ref[8:8 + CHUNK, 0:D_INNER] = x_ref[...]
    ext_ref[8:8 + CHUNK, D_INNER:D_INNER + BC_DIM] = b_ref[...]
    ext_ref[8:8 + CHUNK, D_INNER + BC_DIM:CONV_DIM] = c_ref[...]
    for cc in range(CONV_DIM // 512):
        cs = slice(cc * 512, (cc + 1) * 512)
        acc = cw_ref[4:5, cs] + ext_ref[5:5 + CHUNK, cs] * cw_ref[0:1, cs]
        for k in range(1, CONV_W):
            acc = acc + ext_ref[5 + k:5 + k + CHUNK, cs] * cw_ref[k:k + 1, cs]
        xbc_ref[:, cs] = _silu(acc)
    ext_ref[0:8, :] = ext_ref[CHUNK:CHUNK + 8, :]

    xs_ref = xbc_ref.at[:, 0:D_INNER]
    bm_ref = xbc_ref.at[:, D_INNER:D_INNER + BC_DIM]
    cm_ref = xbc_ref.at[:, D_INNER + BC_DIM:CONV_DIM]
    e_acum, e_end = _ssd_intra(xs_ref, bm_ref, cm_ref, dt_ref[...], par_ref, tri_ref, seg_ref, y_ref, xddt_ref)

    lane = lax.broadcasted_iota(jnp.int32, (CHUNK, LANES), 1)
    gw = SSD_STATE * 2
    for grp in range(SSD_GROUPS):
        glo = grp * SSD_STATE
        rows = slice(grp * gw, (grp + 1) * gw)
        prev = st_ref[rows, :]
        y_off = _dot_nt(cm_ref[:, glo:glo + SSD_STATE].astype(BF16), prev.astype(BF16))
        for half in range(2):
            pair = 2 * grp + half
            plo = pair * LANES
            y_ref[:, plo:plo + LANES] += y_off[:, half * LANES:(half + 1) * LANES] * _pair_cols(e_acum, pair, lane)
        new = _dot(xddt_ref[rows, :].astype(BF16), bm_ref[:, glo:glo + SSD_STATE].astype(BF16))
        for k in range(4):
            h = 4 * grp + k
            hr = slice(grp * gw + k * SSD_HEADDIM, grp * gw + (k + 1) * SSD_HEADDIM)
            dec = jnp.broadcast_to(jnp.broadcast_to(e_end[:, h:h + 1], (CHUNK, SSD_STATE))[0:1, :],
                                   (SSD_HEADDIM, SSD_STATE))
            st_ref[hr, :] = prev[k * SSD_HEADDIM:(k + 1) * SSD_HEADDIM, :] * dec + new[k * SSD_HEADDIM:(k + 1) * SSD_HEADDIM, :]

    _ssd_finish(y_ref, z_ref, par_ref, o_ref)


def _ssd_consts(segment):
    r = jnp.arange(CHUNK)
    same = (r[:, None] // segment) == (r[None, :] // segment)
    return (same & (r[None, :] <= r[:, None])).astype(F32), same.astype(F32)


def _ssd_prompt(u, dt_raw, conv_wb, par, b, s):
    assert s % CHUNK == 0
    tri, seg = _ssd_consts(CHUNK)
    u3 = u.reshape(b, s, U_A)
    col = lambda width, blk: pl.BlockSpec((None, CHUNK, width), lambda i, c: (i, c, blk))
    y, st = pl.pallas_call(
        _ssd_prompt_body,
        grid=(b, s // CHUNK),
        in_specs=[col(D_INNER, 0), col(D_INNER, 1), col(BC_DIM, 2 * D_INNER // BC_DIM),
                  col(BC_DIM, 2 * D_INNER // BC_DIM + 1), col(LANES, 0),
                  _const_spec((8, CONV_DIM)), _const_spec((4, D_INNER)),
                  _const_spec((CHUNK, CHUNK)), _const_spec((CHUNK, CHUNK))],
        out_specs=[pl.BlockSpec((None, CHUNK, D_INNER), lambda i, c: (i, c, 0)),
                   pl.BlockSpec((None, D_INNER, SSD_STATE), lambda i, c: (i, 0, 0))],
        out_shape=[jax.ShapeDtypeStruct((b, s, D_INNER), F32),
                   jax.ShapeDtypeStruct((b, D_INNER, SSD_STATE), F32)],
        scratch_shapes=[pltpu.VMEM((CHUNK + 8, CONV_DIM), F32), pltpu.VMEM((CHUNK, CONV_DIM), F32),
                        pltpu.VMEM((CHUNK, D_INNER), F32), pltpu.VMEM((D_INNER, CHUNK), F32)],
        compiler_params=_params("parallel", "arbitrary"),
        name="ssd_prompt",
    )(u3, u3, u3, u3, dt_raw.reshape(b, s, LANES), conv_wb, par, tri, seg)
    return y.reshape(b * s, D_INNER), st


def _conv_sample_body(seq_ref, cw_ref, o_ref, *, tlen):
    acc = cw_ref[4:5, :] + seq_ref[:, 0:tlen, :] * cw_ref[0:1, :]
    for k in range(1, CONV_W):
        acc = acc + seq_ref[:, k:k + tlen, :] * cw_ref[k:k + 1, :]
    o_ref[...] = _silu(acc)


def _conv_sample(seq, conv_wb, tlen):
    b = seq.shape[0]
    bt = 8
    return pl.pallas_call(
        functools.partial(_conv_sample_body, tlen=tlen),
        grid=(b // bt,),
        in_specs=[pl.BlockSpec((bt, CONV_W - 1 + tlen, CONV_DIM), lambda i: (i, 0, 0)),
                  _const_spec((8, CONV_DIM))],
        out_specs=pl.BlockSpec((bt, tlen, CONV_DIM), lambda i: (i, 0, 0)),
        out_shape=jax.ShapeDtypeStruct((b, tlen, CONV_DIM), F32),
        compiler_params=_params("parallel"),
        name="conv_sample",
    )(seq, conv_wb)


SSD_BT = 8


def _ssd_sample_body(z_ref, xbc_in_ref, dt_ref, par_ref, tri_ref, seg_ref, st_in_ref,
                     o_ref, st_out_ref, xbc_ref, dtp_ref, zp_ref, y_ref, xddt_ref, op_ref, *, tlen):
    rows = SSD_BT * tlen
    xbc_ref[...] = jnp.zeros_like(xbc_ref)
    dtp_ref[...] = jnp.zeros_like(dtp_ref)
    zp_ref[...] = jnp.zeros_like(zp_ref)
    xbc_ref[0:rows, :] = xbc_in_ref[...]
    dtp_ref[0:rows, :] = dt_ref[...]
    zp_ref[0:rows, :] = z_ref[...]
    xs_ref = xbc_ref.at[:, 0:D_INNER]
    bm_ref = xbc_ref.at[:, D_INNER:D_INNER + BC_DIM]
    cm_ref = xbc_ref.at[:, D_INNER + BC_DIM:CONV_DIM]
    e_acum, e_end = _ssd_intra(xs_ref, bm_ref, cm_ref, dtp_ref[...], par_ref, tri_ref, seg_ref, y_ref, xddt_ref)

    lane = lax.broadcasted_iota(jnp.int32, (CHUNK, LANES), 1)
    row_b = lax.broadcasted_iota(jnp.int32, (CHUNK, 1), 0) // tlen
    win = 16
    per_win = win // tlen
    win_b = lax.broadcasted_iota(jnp.int32, (win, 1), 0) // tlen
    gw = SSD_STATE * 2
    e_end_b = [jnp.broadcast_to(e_end[:, h:h + 1], (CHUNK, SSD_STATE)) for h in range(SSD_HEADS)]
    for grp in range(SSD_GROUPS):
        glo = grp * SSD_STATE
        rsl = slice(grp * gw, (grp + 1) * gw)
        c16 = cm_ref[:, glo:glo + SSD_STATE].astype(BF16)
        b32 = bm_ref[:, glo:glo + SSD_STATE]
        xddt16 = xddt_ref[rsl, :].astype(BF16)
        for w in range(rows // win):
            c_win = c16[w * win:(w + 1) * win, :]
            y_off = jnp.zeros((win, gw), F32)
            for k in range(per_win):
                bb = w * per_win + k
                prev = st_in_ref[bb, rsl, :]
                y_off = jnp.where(win_b == k, _dot_nt(c_win, prev.astype(BF16)), y_off)
                new = _dot(xddt16, jnp.where(row_b == bb, b32, 0.0).astype(BF16))
                for hh in range(4):
                    h = 4 * grp + hh
                    hs = slice(hh * SSD_HEADDIM, (hh + 1) * SSD_HEADDIM)
                    dec = jnp.broadcast_to(e_end_b[h][bb * tlen:bb * tlen + 1, :], (SSD_HEADDIM, SSD_STATE))
                    st_out_ref[bb, grp * gw + hh * SSD_HEADDIM:grp * gw + (hh + 1) * SSD_HEADDIM, :] = (
                        prev[hs, :] * dec + new[hs, :])
            for half in range(2):
                pair = 2 * grp + half
                plo = pair * LANES
                scale = _pair_cols(e_acum, pair, lane)[w * win:(w + 1) * win, :]
                y_ref[w * win:(w + 1) * win, plo:plo + LANES] += y_off[:, half * LANES:(half + 1) * LANES] * scale

    _ssd_finish(y_ref, zp_ref, par_ref, op_ref)
    o_ref[...] = op_ref[0:rows, :]


def _ssd_sample(u, xbc, dt_raw, par, state, b, tlen):
    rows = SSD_BT * tlen
    assert CHUNK % rows == 0 and 16 % tlen == 0 and rows % 16 == 0
    tri, seg = _ssd_consts(tlen)
    row = lambda width, blk: pl.BlockSpec((rows, width), lambda i: (i, blk))
    st_spec = pl.BlockSpec((SSD_BT, D_INNER, SSD_STATE), lambda i: (i, 0, 0))
    y, st = pl.pallas_call(
        functools.partial(_ssd_sample_body, tlen=tlen),
        grid=(b // SSD_BT,),
        in_specs=[row(D_INNER, 0), row(CONV_DIM, 0), row(LANES, 0), _const_spec((4, D_INNER)),
                  _const_spec((CHUNK, CHUNK)), _const_spec((CHUNK, CHUNK)), st_spec],
        out_specs=[row(D_INNER, 0), st_spec],
        out_shape=[jax.ShapeDtypeStruct((b * tlen, D_INNER), F32),
                   jax.ShapeDtypeStruct((b, D_INNER, SSD_STATE), F32)],
        scratch_shapes=[pltpu.VMEM((CHUNK, CONV_DIM), F32), pltpu.VMEM((CHUNK, LANES), F32),
                        pltpu.VMEM((CHUNK, D_INNER), F32), pltpu.VMEM((CHUNK, D_INNER), F32),
                        pltpu.VMEM((D_INNER, CHUNK), F32), pltpu.VMEM((CHUNK, D_INNER), F32)],
        compiler_params=_params("parallel"),
        name="ssd_sample",
    )(u, xbc, dt_raw, par, tri, seg, state)
    return y, st


def _prep_weights(norm_g, w_ffn_gu, w_ffn_down, w_in_a, conv_w, conv_b, dt_bias, a_log, d_skip, ssd_norm_g,
                  w_out_a, w_in_b, w_out_b, kv_norm_g, w_kv_shared, mem_norm_g, w_mem_kv, final_norm_g):
    p = {}
    p["g_ffn"] = [[jnp.stack([norm_g[l, 0 + 4 * k], norm_g[l, 1 + 4 * k], final_norm_g]) for k in range(2)]
                  for l in range(DEPTH)]
    p["w_gu"] = w_ffn_gu.astype(BF16)
    p["w_down"] = w_ffn_down.astype(BF16)
    p["w_in_a"] = jnp.concatenate([w_in_a[:, :, :SSD_MAIN], w_in_a[:, :, SSD_IN:]], axis=-1).astype(BF16)
    p["w_dt"] = jnp.pad(w_in_a[:, :, SSD_MAIN:SSD_IN], ((0, 0), (0, 0), (0, LANES - SSD_HEADS))).astype(BF16)
    p["conv_wb"] = jnp.concatenate([conv_w, conv_b[:, None, :], jnp.zeros((N_A, 3, CONV_DIM), F32)], axis=1)
    lane_pad = lambda v: jnp.pad(v, ((0, 0), (0, D_INNER - v.shape[1])))
    p["ssd_par"] = jnp.stack([lane_pad(dt_bias), lane_pad(a_log), jnp.repeat(d_skip, SSD_HEADDIM, axis=1),
                              ssd_norm_g], axis=1)
    p["w_out_a"] = w_out_a.astype(BF16)
    q_dil = N_DIL * DIL_GW
    p["w_in_b"] = jnp.concatenate([w_in_b[:, :, q_dil:], w_in_b[:, :, :q_dil]], axis=-1).astype(BF16)
    p["w_out_b"] = w_out_b.astype(BF16)
    p["w_kv"] = w_kv_shared.astype(BF16)
    p["w_mem_kv"] = w_mem_kv.astype(BF16)
    n = N_DIL * DIL_QH
    p["slopes"] = jnp.exp2(-8.0 * jnp.arange(1, n + 1, dtype=F32) / n)
    p["norm_g"] = norm_g
    p["kv_norm_g"] = kv_norm_g
    p["mem_norm_g"] = mem_norm_g
    return p


def _trunk(p, x, b, s, mem_kv, conv_prev, ssm_prev, win_bufs):
    prompt = win_bufs is None
    ssm_new, conv_new = [], []
    kv = None
    for l in range(DEPTH):
        g = p["norm_g"][l]
        x = _ffn(x, p["g_ffn"][l][0], p["w_gu"][l, 0], p["w_down"][l, 0])
        if l < N_A:
            u = _norm_matmul(x, g[2], p["w_in_a"][l])
            dt_raw = _norm_matmul(x, g[2], p["w_dt"][l])
            raw = u[:, D_INNER:SSD_MAIN].reshape(b, s, CONV_DIM)
            if prompt:
                y_mix, st = _ssd_prompt(u, dt_raw, p["conv_wb"][l], p["ssd_par"][l], b, s)
                conv_new.append(raw[:, s - (CONV_W - 1):])
                mem_o = _mem_attn_prompt(u, SSD_MAIN // MEM_Q, mem_kv[l], b, s)
            else:
                seq = jnp.concatenate([conv_prev[l], raw], axis=1)
                xbc = _conv_sample(seq, p["conv_wb"][l], s).reshape(b * s, CONV_DIM)
                y_mix, st = _ssd_sample(u, xbc, dt_raw, p["ssd_par"][l],
                                        ssm_prev[l].reshape(b, D_INNER, SSD_STATE), b, s)
                conv_new.append(seq[:, s:])
                mem_o = _mem_attn_sample(u, SSD_MAIN // MEM_Q, mem_kv[l], b, s)
            ssm_new.append(st.reshape(b, SSD_HEADS, SSD_HEADDIM, SSD_STATE))
            x = _out_proj_a(y_mix, mem_o, x, g[3], p["w_out_a"][l])
        else:
            i = l - N_A
            u = _norm_matmul(x, g[2], p["w_in_b"][i])
            outs, lses = [], []
            for gi in range(N_DIL):
                if prompt:
                    o, lse = _dil_attn_prompt(p["slopes"], u, kv, gi, b, s)
                else:
                    o, lse = _dil_attn_sample(p["slopes"], u, kv, win_bufs[gi], gi, b, s)
                outs.append(o)
                lses.append(lse)
            if prompt:
                mem_o = _mem_attn_prompt(u, 0, mem_kv[l], b, s)
            else:
                mem_o = _mem_attn_sample(u, 0, mem_kv[l], b, s)
            x = _out_proj_b(outs, lses, mem_o, x, g[3], p["w_out_b"][i])
        x = _ffn(x, p["g_ffn"][l][1], p["w_gu"][l, 1], p["w_down"][l, 1], final=(l == DEPTH - 1))
        if l == N_A - 1:
            kv = _norm_matmul(x, p["kv_norm_g"], p["w_kv"])
    return x, jnp.stack(conv_new), jnp.stack(ssm_new), kv


def kernel(x_prompt, x_sample, mem_prompt, cache_mem_kv, state_ssm, state_conv, cache_win_g1, cache_win_g2,
           cache_win_g3, norm_g, w_ffn_gu, w_ffn_down, w_in_a, conv_w, conv_b, dt_bias, a_log, d_skip,
           ssd_norm_g, w_out_a, w_in_b, w_out_b, kv_norm_g, w_kv_shared, mem_norm_g, w_mem_kv, final_norm_g):
    p = _prep_weights(norm_g, w_ffn_gu, w_ffn_down, w_in_a, conv_w, conv_b, dt_bias, a_log, d_skip, ssd_norm_g,
                      w_out_a, w_in_b, w_out_b, kv_norm_g, w_kv_shared, mem_norm_g, w_mem_kv, final_norm_g)
    bp, sp = x_prompt.shape[:2]
    bs, ss = x_sample.shape[:2]

    mem_flat = mem_prompt.reshape(bp * N_MEM, D_MODEL)
    mem_kv_p = jnp.stack([_norm_matmul(mem_flat, mem_norm_g[l], p["w_mem_kv"][l]).reshape(bp, N_MEM, 2 * MEM_Q)
                          for l in range(DEPTH)])
    y_p, conv_p, ssm_p, kv_p = _trunk(p, x_prompt.reshape(bp * sp, D_MODEL), bp, sp, mem_kv_p, None, None, None)
    kv_p = kv_p.reshape(bp, sp, N_DIL, 2, DIL_KVH, DIL_HD)
    win_p = [kv_p[:, sp - min(w, sp):, gi] for gi, (w, _) in enumerate(DIL_CONFIGS)]

    mem_kv_s = cache_mem_kv.reshape(DEPTH, bs, N_MEM, 2 * MEM_Q)
    y_s, conv_s, ssm_s, kv_s = _trunk(p, x_sample.reshape(bs * ss, D_MODEL), bs, ss, mem_kv_s, state_conv,
                                      state_ssm, (cache_win_g1, cache_win_g2, cache_win_g3))
    kv_s = kv_s.reshape(bs, ss, N_DIL, 2, DIL_KVH, DIL_HD)
    return (y_p.reshape(bp, sp, D_MODEL), y_s.reshape(bs, ss, D_MODEL),
            mem_kv_p.reshape(DEPTH, bp, N_MEM, 2, MEM_HEADS, MEM_HD), ssm_p, conv_p, win_p[0], win_p[1], win_p[2],
            ssm_s, conv_s, kv_s[:, :, 0], kv_s[:, :, 1], kv_s[:, :, 2])
```

```python
import functools

import jax
import jax.numpy as jnp
from jax import lax
from jax.experimental import pallas as pl
from jax.experimental.pallas import tpu as pltpu

F32 = jnp.float32
BF16 = jnp.bfloat16
HIGHEST = lax.Precision.HIGHEST

D_MODEL = 1024
DEPTH = 4
N_A = 2
D_FF = 2816
D_INNER = 2048
SSD_HEADDIM = 64
SSD_HEADS = 32
SSD_GROUPS = 8
SSD_STATE = 128
CONV_W = 4
CONV_DIM = D_INNER + 2 * SSD_GROUPS * SSD_STATE
BC_DIM = SSD_GROUPS * SSD_STATE
DIL_CONFIGS = ((128, 1), (512, 4), (2048, 16))
N_DIL = 3
DIL_QH = 4
DIL_KVH = 2
DIL_HD = 128
DIL_NK = 128
DIL_GW = DIL_QH * DIL_HD
N_MEM = 256
MEM_HEADS = 4
MEM_HD = 256
MEM_Q = MEM_HEADS * MEM_HD
EPS = 1e-6
SSD_MAIN = 2 * D_INNER + 2 * BC_DIM
SSD_IN = SSD_MAIN + SSD_HEADS
U_A = SSD_MAIN + MEM_Q
U_B = MEM_Q + N_DIL * DIL_GW
KV_W = N_DIL * 2 * DIL_KVH * DIL_HD

LANES = 128
CHUNK = 128
FF_CHUNK = 256
VMEM_LIMIT = 56 * 2 ** 20


def _params(*sem):
    return pltpu.CompilerParams(dimension_semantics=sem, vmem_limit_bytes=VMEM_LIMIT)


def _const_spec(shape):
    return pl.BlockSpec(shape, lambda *_: (0,) * len(shape), pipeline_mode=pl.Buffered(1))


def _rms(x, g):
    return x * lax.rsqrt(jnp.mean(x * x, axis=-1, keepdims=True) + EPS) * g


def _silu(x):
    return x / (1.0 + jnp.exp(-x))


def _dot(a, b):
    return jnp.dot(a, b, preferred_element_type=F32)


def _dot_nt(a, b):
    return lax.dot_general(a, b, (((1,), (1,)), ((), ())), preferred_element_type=F32)


def _row_tile(t, pref):
    return pref if t % pref == 0 else t


def _ffn_body(x_ref, g_ref, wgu_ref, wd_ref, o_ref, acc_ref, *, final):
    x = x_ref[...]
    h = _rms(x, g_ref[0:1, :]).astype(BF16)
    for c in range(D_FF // FF_CHUNK):
        lo = c * FF_CHUNK
        gate = _dot(h, wgu_ref[:, lo:lo + FF_CHUNK])
        up = _dot(h, wgu_ref[:, D_FF + lo:D_FF + lo + FF_CHUNK])
        part = _dot((_silu(gate) * up).astype(BF16), wd_ref[lo:lo + FF_CHUNK, :])
        if c == 0:
            acc_ref[...] = part
        else:
            acc_ref[...] += part
    y = x + 0.5 * _rms(acc_ref[...], g_ref[1:2, :])
    if final:
        y = _rms(y, g_ref[2:3, :])
    o_ref[...] = y


def _ffn(x, g3, wgu, wd, final=False):
    t = x.shape[0]
    tm = _row_tile(t, 512)
    return pl.pallas_call(
        functools.partial(_ffn_body, final=final),
        grid=(t // tm,),
        in_specs=[pl.BlockSpec((tm, D_MODEL), lambda i: (i, 0)),
                  _const_spec((3, D_MODEL)),
                  _const_spec((D_MODEL, 2 * D_FF)),
                  _const_spec((D_FF, D_MODEL))],
        out_specs=pl.BlockSpec((tm, D_MODEL), lambda i: (i, 0)),
        out_shape=jax.ShapeDtypeStruct((t, D_MODEL), F32),
        scratch_shapes=[pltpu.VMEM((tm, D_MODEL), F32)],
        compiler_params=_params("parallel"),
        name="ffn",
    )(x, g3, wgu, wd)


def _nmm_body(x_ref, g_ref, w_ref, o_ref, h_ref):
    @pl.when(pl.program_id(1) == 0)
    def _():
        h_ref[...] = _rms(x_ref[...], g_ref[...]).astype(BF16)

    o_ref[...] = _dot(h_ref[...], w_ref[...])


def _norm_matmul(x, g, w):
    t, n = x.shape[0], w.shape[1]
    tm = _row_tile(t, 1024)
    tn = 512 if n % 512 == 0 else n
    return pl.pallas_call(
        _nmm_body,
        grid=(t // tm, n // tn),
        in_specs=[pl.BlockSpec((tm, D_MODEL), lambda i, j: (i, 0)),
                  pl.BlockSpec((1, D_MODEL), lambda i, j: (0, 0)),
                  pl.BlockSpec((D_MODEL, tn), lambda i, j: (0, j))],
        out_specs=pl.BlockSpec((tm, tn), lambda i, j: (i, j)),
        out_shape=jax.ShapeDtypeStruct((t, n), F32),
        scratch_shapes=[pltpu.VMEM((tm, D_MODEL), BF16)],
        compiler_params=_params("parallel", "arbitrary"),
        name="norm_matmul",
    )(x, g.reshape(1, D_MODEL), w)


def _out_a_body(y_ref, m_ref, x_ref, g_ref, w_ref, o_ref):
    acc = _dot(y_ref[...].astype(BF16), w_ref[0:D_INNER, :])
    acc += _dot(m_ref[...].astype(BF16), w_ref[D_INNER:D_INNER + MEM_Q, :])
    o_ref[...] = x_ref[...] + _rms(acc, g_ref[...])


def _out_proj_a(y, m, x, g, w):
    t = x.shape[0]
    tm = _row_tile(t, 512)
    row = lambda width: pl.BlockSpec((tm, width), lambda i: (i, 0))
    return pl.pallas_call(
        _out_a_body,
        grid=(t // tm,),
        in_specs=[row(D_INNER), row(MEM_Q), row(D_MODEL), _const_spec((1, D_MODEL)),
                  _const_spec((D_INNER + MEM_Q, D_MODEL))],
        out_specs=row(D_MODEL),
        out_shape=jax.ShapeDtypeStruct((t, D_MODEL), F32),
        compiler_params=_params("parallel"),
        name="out_proj_a",
    )(y, m, x, g.reshape(1, D_MODEL), w)


def _out_b_body(o1_ref, o2_ref, o3_ref, l1_ref, l2_ref, l3_ref, m_ref, x_ref, g_ref, w_ref, o_ref):
    l1, l2, l3 = l1_ref[...], l2_ref[...], l3_ref[...]
    mx = jnp.maximum(jnp.maximum(l1, l2), l3)
    e1, e2, e3 = jnp.exp(l1 - mx), jnp.exp(l2 - mx), jnp.exp(l3 - mx)
    den = e1 + e2 + e3
    mix = (e1 / den) * o1_ref[...] + (e2 / den) * o2_ref[...] + (e3 / den) * o3_ref[...]
    acc = _dot(mix.astype(BF16), w_ref[0:DIL_GW, :])
    acc += _dot(m_ref[...].astype(BF16), w_ref[DIL_GW:DIL_GW + MEM_Q, :])
    o_ref[...] = x_ref[...] + _rms(acc, g_ref[...])


def _out_proj_b(outs, lses, m, x, g, w):
    t = x.shape[0]
    tm = _row_tile(t, 512)
    row = lambda width: pl.BlockSpec((tm, width), lambda i: (i, 0))
    return pl.pallas_call(
        _out_b_body,
        grid=(t // tm,),
        in_specs=[row(DIL_GW)] * 6 + [row(MEM_Q), row(D_MODEL), _const_spec((1, D_MODEL)),
                                      _const_spec((DIL_GW + MEM_Q, D_MODEL))],
        out_specs=row(D_MODEL),
        out_shape=jax.ShapeDtypeStruct((t, D_MODEL), F32),
        compiler_params=_params("parallel"),
        name="out_proj_b",
    )(*outs, *lses, m, x, g.reshape(1, D_MODEL), w)


def _softmax_rows(s):
    e = jnp.exp(s - jnp.max(s, axis=-1, keepdims=True))
    return e / jnp.sum(e, axis=-1, keepdims=True)


def _mem_prompt_body(q_ref, kv_ref, o_ref):
    for h in range(MEM_HEADS):
        lo = h * MEM_HD
        q = q_ref[:, lo:lo + MEM_HD].astype(BF16)
        k = kv_ref[:, lo:lo + MEM_HD].astype(BF16)
        v = kv_ref[:, MEM_Q + lo:MEM_Q + lo + MEM_HD].astype(BF16)
        p = _softmax_rows(_dot_nt(q, k) * (MEM_HD ** -0.5))
        o_ref[:, lo:lo + MEM_HD] = _dot(p.astype(BF16), v)


def _mem_attn_prompt(u, q_block, mem_kv, layer, b, s):
    c = u.shape[1]
    tq = _row_tile(s, 512)
    out = pl.pallas_call(
        _mem_prompt_body,
        grid=(b, s // tq),
        in_specs=[pl.BlockSpec((None, tq, MEM_Q), lambda i, j: (i, j, q_block)),
                  pl.BlockSpec((None, None, N_MEM, 2 * MEM_Q), lambda i, j: (layer, i, 0, 0))],
        out_specs=pl.BlockSpec((None, tq, MEM_Q), lambda i, j: (i, j, 0)),
        out_shape=jax.ShapeDtypeStruct((b, s, MEM_Q), F32),
        compiler_params=_params("parallel", "parallel"),
        name="mem_attn_prompt",
    )(u.reshape(b, s, c), mem_kv)
    return out.reshape(b * s, MEM_Q)


MEM_BT = 4
MEM_LT = MEM_HD // LANES
MEM_ROWS = 2 * MEM_LT * MEM_HEADS


def _mem_rows(kv_ref, bb, kv, h):
    parts = [kv_ref[bb, pl.ds((kv * MEM_LT + lt) * MEM_HEADS + h, N_MEM, stride=MEM_ROWS), :]
             for lt in range(MEM_LT)]
    return jnp.concatenate(parts, axis=1)


def _mem_sample_body(q_ref, kv_ref, o_ref, *, tlen):
    rows = MEM_BT * tlen
    owner = lax.broadcasted_iota(jnp.int32, (rows, 1), 0) // tlen
    q_all = q_ref[...]
    for h in range(MEM_HEADS):
        lo = h * MEM_HD
        q = q_all[:, lo:lo + MEM_HD].astype(BF16)
        s = jnp.zeros((rows, N_MEM), F32)
        for bb in range(MEM_BT):
            k = _mem_rows(kv_ref, bb, 0, h).astype(BF16)
            s = jnp.where(owner == bb, _dot_nt(q, k), s)
        p = _softmax_rows(s * (MEM_HD ** -0.5)).astype(BF16)
        o = jnp.zeros((rows, MEM_HD), F32)
        for bb in range(MEM_BT):
            v = _mem_rows(kv_ref, bb, 1, h).astype(BF16)
            o = jnp.where(owner == bb, _dot(p, v), o)
        o_ref[:, lo:lo + MEM_HD] = o


def _mem_attn_sample(u, q_block, mem_kv, layer, b, tlen):
    rows = MEM_BT * tlen
    depth = mem_kv.shape[0]
    mem_kv = mem_kv.reshape(depth, b, N_MEM, 2, MEM_HEADS, MEM_LT, LANES).transpose(0, 1, 2, 3, 5, 4, 6)
    mem_kv = mem_kv.reshape(depth, b, N_MEM * MEM_ROWS, LANES)
    return pl.pallas_call(
        functools.partial(_mem_sample_body, tlen=tlen),
        grid=(b // MEM_BT,),
        in_specs=[pl.BlockSpec((rows, MEM_Q), lambda i: (i, q_block)),
                  pl.BlockSpec((None, MEM_BT, N_MEM * MEM_ROWS, LANES), lambda i: (layer, i, 0, 0))],
        out_specs=pl.BlockSpec((rows, MEM_Q), lambda i: (i, 0)),
        out_shape=jax.ShapeDtypeStruct((b * tlen, MEM_Q), F32),
        compiler_params=_params("parallel"),
        name="mem_attn_sample",
    )(u, mem_kv)


def _dil_prompt_body(sl_ref, q_ref, kp_ref, vp_ref, kc_ref, vc_ref, o_ref, l_ref, *, gi, dil):
    has_prev = pl.program_id(2) > 0
    qi = lax.broadcasted_iota(jnp.int32, (DIL_NK, DIL_NK), 0)
    kj = lax.broadcasted_iota(jnp.int32, (DIL_NK, DIL_NK), 1)
    du_prev = DIL_NK + qi - kj
    du_own = qi - kj
    valid_prev = (du_prev <= DIL_NK) & has_prev
    valid_own = du_own >= 0
    neg_slope = -sl_ref[gi * DIL_QH + pl.program_id(1)]
    bias_prev = neg_slope * (dil * du_prev).astype(F32)
    bias_own = neg_slope * (dil * du_own).astype(F32)
    scale = DIL_HD ** -0.5

    def one_stream(sel):
        q = q_ref[sel, :].astype(BF16)
        s_prev = jnp.where(valid_prev, _dot_nt(q, kp_ref[sel, :].astype(BF16)) * scale + bias_prev, -jnp.inf)
        s_own = jnp.where(valid_own, _dot_nt(q, kc_ref[sel, :].astype(BF16)) * scale + bias_own, -jnp.inf)
        mx = jnp.maximum(jnp.max(s_prev, axis=-1, keepdims=True), jnp.max(s_own, axis=-1, keepdims=True))
        e_prev = jnp.exp(s_prev - mx)
        e_own = jnp.exp(s_own - mx)
        den = jnp.sum(e_prev, axis=-1, keepdims=True) + jnp.sum(e_own, axis=-1, keepdims=True)
        o_ref[sel, :] = (_dot((e_prev / den).astype(BF16), vp_ref[sel, :].astype(BF16))
                         + _dot((e_own / den).astype(BF16), vc_ref[sel, :].astype(BF16)))
        l_ref[sel, :] = jnp.broadcast_to(mx + jnp.log(den), (DIL_NK, DIL_HD))

    if dil == 1:
        one_stream(slice(None))
    else:
        def per_residue(r, carry):
            one_stream(pl.ds(r, DIL_NK, stride=dil))
            return carry

        lax.fori_loop(0, dil, per_residue, 0)


def _dil_attn_prompt(slopes, u, kv, gi, b, s):
    window, dil = DIL_CONFIGS[gi]
    tb = DIL_NK * dil
    assert window // dil == DIL_NK and s % tb == 0
    qpk = DIL_QH // DIL_KVH
    q0 = (MEM_Q + gi * DIL_GW) // DIL_HD
    k0 = gi * DIL_GW // DIL_HD
    v0 = k0 + DIL_KVH
    blk = lambda imap: pl.BlockSpec((None, tb, DIL_HD), imap)
    u3 = u.reshape(b, s, U_B)
    kv3 = kv.reshape(b, s, KV_W)
    o, lse = pl.pallas_call(
        functools.partial(_dil_prompt_body, gi=gi, dil=dil),
        grid=(b, DIL_QH, s // tb),
        in_specs=[pl.BlockSpec(memory_space=pltpu.SMEM),
                  blk(lambda i, h, j: (i, j, q0 + h)),
                  blk(lambda i, h, j: (i, jnp.maximum(j - 1, 0), k0 + h // qpk)),
                  blk(lambda i, h, j: (i, jnp.maximum(j - 1, 0), v0 + h // qpk)),
                  blk(lambda i, h, j: (i, j, k0 + h // qpk)),
                  blk(lambda i, h, j: (i, j, v0 + h // qpk))],
        out_specs=[blk(lambda i, h, j: (i, j, h))] * 2,
        out_shape=[jax.ShapeDtypeStruct((b, s, DIL_GW), F32)] * 2,
        compiler_params=_params("parallel", "parallel", "arbitrary"),
        name=f"dil_attn_prompt_g{gi}",
    )(slopes, u3, kv3, kv3, kv3, kv3)
    return o.reshape(b * s, DIL_GW), lse.reshape(b * s, DIL_GW)


DIL_BT = 4


def _dil_sample_body(sl_ref, q_ref, buf_ref, new_ref, o_ref, l_ref, *, gi, dil, tlen):
    mi = lax.broadcasted_iota(jnp.int32, (DIL_NK, 1), 0)
    si = lax.broadcasted_iota(jnp.int32, (tlen, 1), 0)
    scale = DIL_HD ** -0.5
    per_pos = dil * 2 * DIL_KVH

    def per_batch(bb, carry):
        q_all = q_ref[bb]
        new = new_ref[bb]
        for t in range(tlen):
            if dil == 1:
                res = 0
                j_buf, ok_buf = DIL_NK + t - mi, mi >= t
                j_new, ok_new = t - si, si <= t
            else:
                res = t
                j_buf, ok_buf = DIL_NK - mi, mi >= 0
                j_new, ok_new = jnp.zeros_like(si), si == t
            dist_buf = (dil * j_buf).astype(F32)
            dist_new = (dil * j_new).astype(F32)
            for kvh in range(DIL_KVH):
                klo = kvh * DIL_HD
                vlo = DIL_KVH * DIL_HD + klo
                k_buf = buf_ref[bb, pl.ds((res * 2 + 0) * DIL_KVH + kvh, DIL_NK, stride=per_pos), :]
                v_buf = buf_ref[bb, pl.ds((res * 2 + 1) * DIL_KVH + kvh, DIL_NK, stride=per_pos), :]
                k_new = new[:, klo:klo + DIL_HD]
                v_new = new[:, vlo:vlo + DIL_HD]
                for qh in range(DIL_QH // DIL_KVH):
                    head = kvh * (DIL_QH // DIL_KVH) + qh
                    lo = head * DIL_HD
                    neg_slope = -sl_ref[gi * DIL_QH + head]
                    q = q_all[t:t + 1, lo:lo + DIL_HD]
                    s_buf = jnp.sum(q * k_buf, axis=-1, keepdims=True) * scale + neg_slope * dist_buf
                    s_new = jnp.sum(q * k_new, axis=-1, keepdims=True) * scale + neg_slope * dist_new
                    s_buf = jnp.where(ok_buf, s_buf, -jnp.inf)
                    s_new = jnp.where(ok_new, s_new, -jnp.inf)
                    mx = jnp.maximum(jnp.max(s_buf, axis=0, keepdims=True), jnp.max(s_new, axis=0, keepdims=True))
                    e_buf = jnp.exp(s_buf - mx)
                    e_new = jnp.exp(s_new - mx)
                    den = jnp.sum(e_buf, axis=0, keepdims=True) + jnp.sum(e_new, axis=0, keepdims=True)
                    acc = jnp.sum(e_buf * v_buf, axis=0, keepdims=True) + jnp.sum(e_new * v_new, axis=0, keepdims=True)
                    o_ref[bb, t:t + 1, lo:lo + DIL_HD] = acc / den
                    l_ref[bb, t:t + 1, lo:lo + DIL_HD] = jnp.broadcast_to(mx + jnp.log(den), (1, DIL_HD))
        return carry

    lax.fori_loop(0, DIL_BT, per_batch, 0)


def _dil_attn_sample(slopes, u, kv_new, win_buf, gi, b, tlen):
    window, dil = DIL_CONFIGS[gi]
    lb = win_buf.shape[1]
    assert window // dil == DIL_NK and lb == window and (dil == 1 or tlen <= dil)
    per_pos = dil * 2 * DIL_KVH
    buf = win_buf.reshape(b, DIL_NK * per_pos, DIL_HD)
    buf_spec = pl.BlockSpec((DIL_BT, DIL_NK * per_pos, DIL_HD), lambda i: (i, 0, 0))
    o, lse = pl.pallas_call(
        functools.partial(_dil_sample_body, gi=gi, dil=dil, tlen=tlen),
        grid=(b // DIL_BT,),
        in_specs=[pl.BlockSpec(memory_space=pltpu.SMEM),
                  pl.BlockSpec((DIL_BT, tlen, DIL_GW), lambda i: (i, 0, MEM_Q // DIL_GW + gi)),
                  buf_spec,
                  pl.BlockSpec((DIL_BT, tlen, DIL_GW), lambda i: (i, 0, gi))],
        out_specs=[pl.BlockSpec((DIL_BT, tlen, DIL_GW), lambda i: (i, 0, 0))] * 2,
        out_shape=[jax.ShapeDtypeStruct((b, tlen, DIL_GW), F32)] * 2,
        compiler_params=_params("parallel"),
        name=f"dil_attn_sample_g{gi}",
    )(slopes, u.reshape(b, tlen, U_B), buf, kv_new.reshape(b, tlen, KV_W))
    return o.reshape(b * tlen, DIL_GW), lse.reshape(b * tlen, DIL_GW)


def _pair_cols(a, pair, lane):
    h0 = 2 * pair
    return jnp.where(lane < SSD_HEADDIM, a[:, h0:h0 + 1], a[:, h0 + 1:h0 + 2])


def _ssd_intra(xs_ref, b_ref, c_ref, dt_raw, par_ref, tri_ref, seg_ref, y_ref, xddt_ref):
    dt_bias, a_log = par_ref[0:1, 0:LANES], par_ref[1:2, 0:LANES]
    pre = dt_raw + dt_bias
    dt = jnp.maximum(pre, 0.0) + jnp.log(1.0 + jnp.exp(-jnp.abs(pre)))
    d_a = dt * -jnp.exp(a_log)
    tri = tri_ref[...]
    acum = jnp.dot(tri, d_a, precision=HIGHEST, preferred_element_type=F32)
    a_end = jnp.dot(seg_ref[...], d_a, precision=HIGHEST, preferred_element_type=F32)
    acum_t = acum.T
    causal = tri > 0.5
    e_acum = jnp.exp(acum)
    e_end = jnp.exp(a_end)
    d_end = jnp.exp(a_end - acum)
    lane = lax.broadcasted_iota(jnp.int32, (CHUNK, LANES), 1)
    for grp in range(SSD_GROUPS):
        glo = grp * SSD_STATE
        cb = _dot_nt(c_ref[:, glo:glo + SSD_STATE].astype(BF16), b_ref[:, glo:glo + SSD_STATE].astype(BF16))
        for half in range(2):
            pair = 2 * grp + half
            plo = pair * LANES
            xs = xs_ref[:, plo:plo + LANES]
            xd = xs * _pair_cols(dt, pair, lane)
            xd16 = xd.astype(BF16)
            yd = []
            for k in range(2):
                h = 2 * pair + k
                decay = jnp.exp(jnp.where(causal, acum[:, h:h + 1] - acum_t[h:h + 1, :], -jnp.inf))
                yd.append(_dot((cb * decay).astype(BF16), xd16))
            y_ref[:, plo:plo + LANES] = jnp.where(lane < SSD_HEADDIM, yd[0], yd[1]) + par_ref[2:3, plo:plo + LANES] * xs
            xddt_ref[plo:plo + LANES, :] = (xd * _pair_cols(d_end, pair, lane)).T
    return e_acum, e_end


def _ssd_finish(y_ref, z_ref, par_ref, o_ref):
    o_ref[...] = _rms(y_ref[...] * _silu(z_ref[...]), par_ref[3:4, :])


def _ssd_prompt_body(z_ref, x_ref, b_ref, c_ref, dt_ref, cw_ref, par_ref, tri_ref, seg_ref,
                     o_ref, st_ref, ext_ref, xbc_ref, y_ref, xddt_ref):
    first = pl.program_id(1) == 0

    @pl.when(first)
    def _():
        ext_ref[0:8, :] = jnp.zeros((8, CONV_DIM), F32)
        st_ref[...] = jnp.zeros_like(st_ref)

    ext_ref[8:8 + CHUNK, 0:D_INNER] = x_ref[...]
    ext_ref[8:8 + CHUNK, D_INNER:D_INNER + BC_DIM] = b_ref[...]
    ext_ref[8:8 + CHUNK, D_INNER + BC_DIM:CONV_DIM] = c_ref[...]
    for cc in range(CONV_DIM // 512):
        cs = slice(cc * 512, (cc + 1) * 512)
        acc = cw_ref[4:5, cs] + ext_ref[5:5 + CHUNK, cs] * cw_ref[0:1, cs]
        for k in range(1, CONV_W):
            acc = acc + ext_ref[5 + k:5 + k + CHUNK, cs] * cw_ref[k:k + 1, cs]
        xbc_ref[:, cs] = _silu(acc)
    ext_ref[0:8, :] = ext_ref[CHUNK:CHUNK + 8, :]

    xs_ref = xbc_ref.at[:, 0:D_INNER]
    bm_ref = xbc_ref.at[:, D_INNER:D_INNER + BC_DIM]
    cm_ref = xbc_ref.at[:, D_INNER + BC_DIM:CONV_DIM]
    e_acum, e_end = _ssd_intra(xs_ref, bm_ref, cm_ref, dt_ref[...], par_ref, tri_ref, seg_ref, y_ref, xddt_ref)

    lane = lax.broadcasted_iota(jnp.int32, (CHUNK, LANES), 1)
    gw = SSD_STATE * 2
    for grp in range(SSD_GROUPS):
        glo = grp * SSD_STATE
        rows = slice(grp * gw, (grp + 1) * gw)
        prev = st_ref[rows, :]
        y_off = _dot_nt(cm_ref[:, glo:glo + SSD_STATE].astype(BF16), prev.astype(BF16))
        for half in range(2):
            pair = 2 * grp + half
            plo = pair * LANES
            y_ref[:, plo:plo + LANES] += y_off[:, half * LANES:(half + 1) * LANES] * _pair_cols(e_acum, pair, lane)
        new = _dot(xddt_ref[rows, :].astype(BF16), bm_ref[:, glo:glo + SSD_STATE].astype(BF16))
        for k in range(4):
            h = 4 * grp + k
            hr = slice(grp * gw + k * SSD_HEADDIM, grp * gw + (k + 1) * SSD_HEADDIM)
            dec = jnp.broadcast_to(jnp.broadcast_to(e_end[:, h:h + 1], (CHUNK, SSD_STATE))[0:1, :],
                                   (SSD_HEADDIM, SSD_STATE))
            st_ref[hr, :] = prev[k * SSD_HEADDIM:(k + 1) * SSD_HEADDIM, :] * dec + new[k * SSD_HEADDIM:(k + 1) * SSD_HEADDIM, :]

    _ssd_finish(y_ref, z_ref, par_ref, o_ref)


def _ssd_consts(segment):
    r = jnp.arange(CHUNK)
    same = (r[:, None] // segment) == (r[None, :] // segment)
    return (same & (r[None, :] <= r[:, None])).astype(F32), same.astype(F32)


def _ssd_prompt(u, dt_raw, conv_wb, par, b, s):
    assert s % CHUNK == 0
    tri, seg = _ssd_consts(CHUNK)
    u3 = u.reshape(b, s, U_A)
    col = lambda width, blk: pl.BlockSpec((None, CHUNK, width), lambda i, c: (i, c, blk))
    y, st = pl.pallas_call(
        _ssd_prompt_body,
        grid=(b, s // CHUNK),
        in_specs=[col(D_INNER, 0), col(D_INNER, 1), col(BC_DIM, 2 * D_INNER // BC_DIM),
                  col(BC_DIM, 2 * D_INNER // BC_DIM + 1), col(LANES, 0),
                  _const_spec((8, CONV_DIM)), _const_spec((4, D_INNER)),
                  _const_spec((CHUNK, CHUNK)), _const_spec((CHUNK, CHUNK))],
        out_specs=[pl.BlockSpec((None, CHUNK, D_INNER), lambda i, c: (i, c, 0)),
                   pl.BlockSpec((None, D_INNER, SSD_STATE), lambda i, c: (i, 0, 0))],
        out_shape=[jax.ShapeDtypeStruct((b, s, D_INNER), F32),
                   jax.ShapeDtypeStruct((b, D_INNER, SSD_STATE), F32)],
        scratch_shapes=[pltpu.VMEM((CHUNK + 8, CONV_DIM), F32), pltpu.VMEM((CHUNK, CONV_DIM), F32),
                        pltpu.VMEM((CHUNK, D_INNER), F32), pltpu.VMEM((D_INNER, CHUNK), F32)],
        compiler_params=_params("parallel", "arbitrary"),
        name="ssd_prompt",
    )(u3, u3, u3, u3, dt_raw.reshape(b, s, LANES), conv_wb, par, tri, seg)
    return y.reshape(b * s, D_INNER), st


def _conv_sample_body(seq_ref, cw_ref, o_ref, *, tlen):
    acc = cw_ref[4:5, :] + seq_ref[:, 0:tlen, :] * cw_ref[0:1, :]
    for k in range(1, CONV_W):
        acc = acc + seq_ref[:, k:k + tlen, :] * cw_ref[k:k + 1, :]
    o_ref[...] = _silu(acc)


def _conv_sample(seq, conv_wb, tlen):
    b = seq.shape[0]
    bt = 8
    return pl.pallas_call(
        functools.partial(_conv_sample_body, tlen=tlen),
        grid=(b // bt,),
        in_specs=[pl.BlockSpec((bt, CONV_W - 1 + tlen, CONV_DIM), lambda i: (i, 0, 0)),
                  _const_spec((8, CONV_DIM))],
        out_specs=pl.BlockSpec((bt, tlen, CONV_DIM), lambda i: (i, 0, 0)),
        out_shape=jax.ShapeDtypeStruct((b, tlen, CONV_DIM), F32),
        compiler_params=_params("parallel"),
        name="conv_sample",
    )(seq, conv_wb)


SSD_BT = 8


def _ssd_sample_body(z_ref, xbc_in_ref, dt_ref, par_ref, tri_ref, seg_ref, st_in_ref,
                     o_ref, st_out_ref, xbc_ref, dtp_ref, zp_ref, y_ref, xddt_ref, op_ref, *, tlen):
    rows = SSD_BT * tlen
    xbc_ref[...] = jnp.zeros_like(xbc_ref)
    dtp_ref[...] = jnp.zeros_like(dtp_ref)
    zp_ref[...] = jnp.zeros_like(zp_ref)
    xbc_ref[0:rows, :] = xbc_in_ref[...]
    dtp_ref[0:rows, :] = dt_ref[...]
    zp_ref[0:rows, :] = z_ref[...]
    xs_ref = xbc_ref.at[:, 0:D_INNER]
    bm_ref = xbc_ref.at[:, D_INNER:D_INNER + BC_DIM]
    cm_ref = xbc_ref.at[:, D_INNER + BC_DIM:CONV_DIM]
    e_acum, e_end = _ssd_intra(xs_ref, bm_ref, cm_ref, dtp_ref[...], par_ref, tri_ref, seg_ref, y_ref, xddt_ref)

    lane = lax.broadcasted_iota(jnp.int32, (CHUNK, LANES), 1)
    row_b = lax.broadcasted_iota(jnp.int32, (CHUNK, 1), 0) // tlen
    win = 16
    per_win = win // tlen
    win_b = lax.broadcasted_iota(jnp.int32, (win, 1), 0) // tlen
    gw = SSD_STATE * 2
    e_end_b = [jnp.broadcast_to(e_end[:, h:h + 1], (CHUNK, SSD_STATE)) for h in range(SSD_HEADS)]
    for grp in range(SSD_GROUPS):
        glo = grp * SSD_STATE
        rsl = slice(grp * gw, (grp + 1) * gw)
        c16 = cm_ref[:, glo:glo + SSD_STATE].astype(BF16)
        b32 = bm_ref[:, glo:glo + SSD_STATE]
        xddt16 = xddt_ref[rsl, :].astype(BF16)
        for w in range(rows // win):
            c_win = c16[w * win:(w + 1) * win, :]
            y_off = jnp.zeros((win, gw), F32)
            for k in range(per_win):
                bb = w * per_win + k
                prev = st_in_ref[bb, rsl, :]
                y_off = jnp.where(win_b == k, _dot_nt(c_win, prev.astype(BF16)), y_off)
                new = _dot(xddt16, jnp.where(row_b == bb, b32, 0.0).astype(BF16))
                for hh in range(4):
                    h = 4 * grp + hh
                    hs = slice(hh * SSD_HEADDIM, (hh + 1) * SSD_HEADDIM)
                    dec = jnp.broadcast_to(e_end_b[h][bb * tlen:bb * tlen + 1, :], (SSD_HEADDIM, SSD_STATE))
                    st_out_ref[bb, grp * gw + hh * SSD_HEADDIM:grp * gw + (hh + 1) * SSD_HEADDIM, :] = (
                        prev[hs, :] * dec + new[hs, :])
            for half in range(2):
                pair = 2 * grp + half
                plo = pair * LANES
                scale = _pair_cols(e_acum, pair, lane)[w * win:(w + 1) * win, :]
                y_ref[w * win:(w + 1) * win, plo:plo + LANES] += y_off[:, half * LANES:(half + 1) * LANES] * scale

    _ssd_finish(y_ref, zp_ref, par_ref, op_ref)
    o_ref[...] = op_ref[0:rows, :]


def _ssd_sample(u, xbc, dt_raw, par, state, layer, b, tlen):
    rows = SSD_BT * tlen
    assert CHUNK % rows == 0 and 16 % tlen == 0 and rows % 16 == 0
    tri, seg = _ssd_consts(tlen)
    row = lambda width, blk: pl.BlockSpec((rows, width), lambda i: (i, blk))
    st_spec = pl.BlockSpec((SSD_BT, D_INNER, SSD_STATE), lambda i: (i, 0, 0))
    st_in_spec = pl.BlockSpec((None, SSD_BT, D_INNER, SSD_STATE), lambda i: (layer, i, 0, 0))
    y, st = pl.pallas_call(
        functools.partial(_ssd_sample_body, tlen=tlen),
        grid=(b // SSD_BT,),
        in_specs=[row(D_INNER, 0), row(CONV_DIM, 0), row(LANES, 0), _const_spec((4, D_INNER)),
                  _const_spec((CHUNK, CHUNK)), _const_spec((CHUNK, CHUNK)), st_in_spec],
        out_specs=[row(D_INNER, 0), st_spec],
        out_shape=[jax.ShapeDtypeStruct((b * tlen, D_INNER), F32),
                   jax.ShapeDtypeStruct((b, D_INNER, SSD_STATE), F32)],
        scratch_shapes=[pltpu.VMEM((CHUNK, CONV_DIM), F32), pltpu.VMEM((CHUNK, LANES), F32),
                        pltpu.VMEM((CHUNK, D_INNER), F32), pltpu.VMEM((CHUNK, D_INNER), F32),
                        pltpu.VMEM((D_INNER, CHUNK), F32), pltpu.VMEM((CHUNK, D_INNER), F32)],
        compiler_params=_params("parallel"),
        name="ssd_sample",
    )(u, xbc, dt_raw, par, tri, seg, state)
    return y, st


def _prep_weights(norm_g, w_ffn_gu, w_ffn_down, w_in_a, conv_w, conv_b, dt_bias, a_log, d_skip, ssd_norm_g,
                  w_out_a, w_in_b, w_out_b, kv_norm_g, w_kv_shared, mem_norm_g, w_mem_kv, final_norm_g):
    p = {}
    p["g_ffn"] = [[jnp.stack([norm_g[l, 0 + 4 * k], norm_g[l, 1 + 4 * k], final_norm_g]) for k in range(2)]
                  for l in range(DEPTH)]
    p["w_gu"] = w_ffn_gu.astype(BF16)
    p["w_down"] = w_ffn_down.astype(BF16)
    p["w_in_a"] = jnp.concatenate([w_in_a[:, :, :SSD_MAIN], w_in_a[:, :, SSD_IN:]], axis=-1).astype(BF16)
    p["w_dt"] = jnp.pad(w_in_a[:, :, SSD_MAIN:SSD_IN], ((0, 0), (0, 0), (0, LANES - SSD_HEADS))).astype(BF16)
    p["conv_wb"] = jnp.concatenate([conv_w, conv_b[:, None, :], jnp.zeros((N_A, 3, CONV_DIM), F32)], axis=1)
    lane_pad = lambda v: jnp.pad(v, ((0, 0), (0, D_INNER - v.shape[1])))
    p["ssd_par"] = jnp.stack([lane_pad(dt_bias), lane_pad(a_log), jnp.repeat(d_skip, SSD_HEADDIM, axis=1),
                              ssd_norm_g], axis=1)
    p["w_out_a"] = w_out_a.astype(BF16)
    q_dil = N_DIL * DIL_GW
    p["w_in_b"] = jnp.concatenate([w_in_b[:, :, q_dil:], w_in_b[:, :, :q_dil]], axis=-1).astype(BF16)
    p["w_out_b"] = w_out_b.astype(BF16)
    p["w_kv"] = w_kv_shared.astype(BF16)
    p["w_mem_kv"] = w_mem_kv.astype(BF16)
    n = N_DIL * DIL_QH
    p["slopes"] = jnp.exp2(-8.0 * jnp.arange(1, n + 1, dtype=F32) / n)
    p["norm_g"] = norm_g
    p["kv_norm_g"] = kv_norm_g
    p["mem_norm_g"] = mem_norm_g
    return p


def _trunk(p, x, b, s, mem_kv, conv_prev, ssm_prev, win_bufs):
    prompt = win_bufs is None
    ssm_new, conv_new = [], []
    kv = None
    for l in range(DEPTH):
        g = p["norm_g"][l]
        x = _ffn(x, p["g_ffn"][l][0], p["w_gu"][l, 0], p["w_down"][l, 0])
        if l < N_A:
            u = _norm_matmul(x, g[2], p["w_in_a"][l])
            dt_raw = _norm_matmul(x, g[2], p["w_dt"][l])
            u3 = u.reshape(b, s, U_A)
            if prompt:
                y_mix, st = _ssd_prompt(u, dt_raw, p["conv_wb"][l], p["ssd_par"][l], b, s)
                conv_new.append(u3[:, s - (CONV_W - 1):, D_INNER:SSD_MAIN])
                mem_o = _mem_attn_prompt(u, SSD_MAIN // MEM_Q, mem_kv, l, b, s)
            else:
                seq = jnp.concatenate([conv_prev[l], u3[:, :, D_INNER:SSD_MAIN]], axis=1)
                xbc = _conv_sample(seq, p["conv_wb"][l], s).reshape(b * s, CONV_DIM)
                y_mix, st = _ssd_sample(u, xbc, dt_raw, p["ssd_par"][l], ssm_prev, l, b, s)
                conv_new.append(seq[:, s:])
                mem_o = _mem_attn_sample(u, SSD_MAIN // MEM_Q, mem_kv, l, b, s)
            ssm_new.append(st.reshape(b, SSD_HEADS, SSD_HEADDIM, SSD_STATE))
            x = _out_proj_a(y_mix, mem_o, x, g[3], p["w_out_a"][l])
        else:
            i = l - N_A
            u = _norm_matmul(x, g[2], p["w_in_b"][i])
            outs, lses = [], []
            for gi in range(N_DIL):
                if prompt:
                    o, lse = _dil_attn_prompt(p["slopes"], u, kv, gi, b, s)
                else:
                    o, lse = _dil_attn_sample(p["slopes"], u, kv, win_bufs[gi], gi, b, s)
                outs.append(o)
                lses.append(lse)
            if prompt:
                mem_o = _mem_attn_prompt(u, 0, mem_kv, l, b, s)
            else:
                mem_o = _mem_attn_sample(u, 0, mem_kv, l, b, s)
            x = _out_proj_b(outs, lses, mem_o, x, g[3], p["w_out_b"][i])
        x = _ffn(x, p["g_ffn"][l][1], p["w_gu"][l, 1], p["w_down"][l, 1], final=(l == DEPTH - 1))
        if l == N_A - 1:
            kv = _norm_matmul(x, p["kv_norm_g"], p["w_kv"])
    return x, jnp.stack(conv_new), jnp.stack(ssm_new), kv


def kernel(x_prompt, x_sample, mem_prompt, cache_mem_kv, state_ssm, state_conv, cache_win_g1, cache_win_g2,
           cache_win_g3, norm_g, w_ffn_gu, w_ffn_down, w_in_a, conv_w, conv_b, dt_bias, a_log, d_skip,
           ssd_norm_g, w_out_a, w_in_b, w_out_b, kv_norm_g, w_kv_shared, mem_norm_g, w_mem_kv, final_norm_g):
    p = _prep_weights(norm_g, w_ffn_gu, w_ffn_down, w_in_a, conv_w, conv_b, dt_bias, a_log, d_skip, ssd_norm_g,
                      w_out_a, w_in_b, w_out_b, kv_norm_g, w_kv_shared, mem_norm_g, w_mem_kv, final_norm_g)
    bp, sp = x_prompt.shape[:2]
    bs, ss = x_sample.shape[:2]

    mem_flat = mem_prompt.reshape(bp * N_MEM, D_MODEL)
    mem_kv_p = jnp.stack([_norm_matmul(mem_flat, mem_norm_g[l], p["w_mem_kv"][l]).reshape(bp, N_MEM, 2 * MEM_Q)
                          for l in range(DEPTH)])
    y_p, conv_p, ssm_p, kv_p = _trunk(p, x_prompt.reshape(bp * sp, D_MODEL), bp, sp, mem_kv_p, None, None, None)
    kv_p = kv_p.reshape(bp, sp, KV_W)
    win_p = [kv_p[:, sp - min(w, sp):, gi * DIL_GW:(gi + 1) * DIL_GW].reshape(bp, min(w, sp), 2, DIL_KVH, DIL_HD)
             for gi, (w, _) in enumerate(DIL_CONFIGS)]

    y_s, conv_s, ssm_s, kv_s = _trunk(p, x_sample.reshape(bs * ss, D_MODEL), bs, ss, cache_mem_kv, state_conv,
                                      state_ssm.reshape(N_A, bs, D_INNER, SSD_STATE),
                                      (cache_win_g1, cache_win_g2, cache_win_g3))
    kv_s = kv_s.reshape(bs, ss, N_DIL, 2, DIL_KVH, DIL_HD)
    return (y_p.reshape(bp, sp, D_MODEL), y_s.reshape(bs, ss, D_MODEL),
            mem_kv_p.reshape(DEPTH, bp, N_MEM, 2, MEM_HEADS, MEM_HD), ssm_p, conv_p, win_p[0], win_p[1], win_p[2],
            ssm_s, conv_s, kv_s[:, :, 0], kv_s[:, :, 1], kv_s[:, :, 2])
```

```python
import functools

import jax
import jax.numpy as jnp
from jax import lax
from jax.experimental import pallas as pl
from jax.experimental.pallas import tpu as pltpu

F32 = jnp.float32
BF16 = jnp.bfloat16
HIGHEST = lax.Precision.HIGHEST

D_MODEL = 1024
DEPTH = 4
N_A = 2
D_FF = 2816
D_INNER = 2048
SSD_HEADDIM = 64
SSD_HEADS = 32
SSD_GROUPS = 8
SSD_STATE = 128
CONV_W = 4
CONV_DIM = D_INNER + 2 * SSD_GROUPS * SSD_STATE
BC_DIM = SSD_GROUPS * SSD_STATE
DIL_CONFIGS = ((128, 1), (512, 4), (2048, 16))
N_DIL = 3
DIL_QH = 4
DIL_KVH = 2
DIL_HD = 128
DIL_NK = 128
DIL_GW = DIL_QH * DIL_HD
N_MEM = 256
MEM_HEADS = 4
MEM_HD = 256
MEM_Q = MEM_HEADS * MEM_HD
EPS = 1e-6
SSD_MAIN = 2 * D_INNER + 2 * BC_DIM
SSD_IN = SSD_MAIN + SSD_HEADS
U_A = SSD_MAIN + MEM_Q
U_B = MEM_Q + N_DIL * DIL_GW
KV_W = N_DIL * 2 * DIL_KVH * DIL_HD

LANES = 128
CHUNK = 128
FF_CHUNK = 256
VMEM_LIMIT = 56 * 2 ** 20


def _params(*sem):
    return pltpu.CompilerParams(dimension_semantics=sem, vmem_limit_bytes=VMEM_LIMIT)


def _const_spec(shape):
    return pl.BlockSpec(shape, lambda *_: (0,) * len(shape), pipeline_mode=pl.Buffered(1))


def _rms(x, g):
    return x * lax.rsqrt(jnp.mean(x * x, axis=-1, keepdims=True) + EPS) * g


def _silu(x):
    return x / (1.0 + jnp.exp(-x))


def _dot(a, b):
    return jnp.dot(a, b, preferred_element_type=F32)


def _dot_nt(a, b):
    return lax.dot_general(a, b, (((1,), (1,)), ((), ())), preferred_element_type=F32)


def _row_tile(t, pref):
    return pref if t % pref == 0 else t


def _ffn_body(x_ref, g_ref, wgu_ref, wd_ref, o_ref, acc_ref, *, final):
    x = x_ref[...]
    h = _rms(x, g_ref[0:1, :]).astype(BF16)
    for c in range(D_FF // FF_CHUNK):
        lo = c * FF_CHUNK
        gate = _dot(h, wgu_ref[:, lo:lo + FF_CHUNK])
        up = _dot(h, wgu_ref[:, D_FF + lo:D_FF + lo + FF_CHUNK])
        part = _dot((_silu(gate) * up).astype(BF16), wd_ref[lo:lo + FF_CHUNK, :])
        if c == 0:
            acc_ref[...] = part
        else:
            acc_ref[...] += part
    y = x + 0.5 * _rms(acc_ref[...], g_ref[1:2, :])
    if final:
        y = _rms(y, g_ref[2:3, :])
    o_ref[...] = y


def _ffn(x, g3, wgu, wd, final=False):
    t = x.shape[0]
    tm = _row_tile(t, 512)
    return pl.pallas_call(
        functools.partial(_ffn_body, final=final),
        grid=(t // tm,),
        in_specs=[pl.BlockSpec((tm, D_MODEL), lambda i: (i, 0)),
                  _const_spec((3, D_MODEL)),
                  _const_spec((D_MODEL, 2 * D_FF)),
                  _const_spec((D_FF, D_MODEL))],
        out_specs=pl.BlockSpec((tm, D_MODEL), lambda i: (i, 0)),
        out_shape=jax.ShapeDtypeStruct((t, D_MODEL), F32),
        scratch_shapes=[pltpu.VMEM((tm, D_MODEL), F32)],
        compiler_params=_params("parallel"),
        name="ffn",
    )(x, g3, wgu, wd)


def _nmm_body(x_ref, g_ref, w_ref, o_ref, h_ref):
    @pl.when(pl.program_id(1) == 0)
    def _():
        h_ref[...] = _rms(x_ref[...], g_ref[...]).astype(BF16)

    o_ref[...] = _dot(h_ref[...], w_ref[...])


def _norm_matmul(x, g, w):
    t, n = x.shape[0], w.shape[1]
    tm = _row_tile(t, 1024)
    tn = n if n <= 2560 else (1024 if n % 1024 == 0 else 512)
    return pl.pallas_call(
        _nmm_body,
        grid=(t // tm, n // tn),
        in_specs=[pl.BlockSpec((tm, D_MODEL), lambda i, j: (i, 0)),
                  pl.BlockSpec((1, D_MODEL), lambda i, j: (0, 0)),
                  pl.BlockSpec((D_MODEL, tn), lambda i, j: (0, j))],
        out_specs=pl.BlockSpec((tm, tn), lambda i, j: (i, j)),
        out_shape=jax.ShapeDtypeStruct((t, n), F32),
        scratch_shapes=[pltpu.VMEM((tm, D_MODEL), BF16)],
        compiler_params=_params("parallel", "arbitrary"),
        name="norm_matmul",
    )(x, g.reshape(1, D_MODEL), w)


def _out_a_body(y_ref, m_ref, x_ref, g_ref, w_ref, o_ref):
    acc = _dot(y_ref[...].astype(BF16), w_ref[0:D_INNER, :])
    acc += _dot(m_ref[...].astype(BF16), w_ref[D_INNER:D_INNER + MEM_Q, :])
    o_ref[...] = x_ref[...] + _rms(acc, g_ref[...])


def _out_proj_a(y, m, x, g, w):
    t = x.shape[0]
    tm = _row_tile(t, 512)
    row = lambda width: pl.BlockSpec((tm, width), lambda i: (i, 0))
    return pl.pallas_call(
        _out_a_body,
        grid=(t // tm,),
        in_specs=[row(D_INNER), row(MEM_Q), row(D_MODEL), _const_spec((1, D_MODEL)),
                  _const_spec((D_INNER + MEM_Q, D_MODEL))],
        out_specs=row(D_MODEL),
        out_shape=jax.ShapeDtypeStruct((t, D_MODEL), F32),
        compiler_params=_params("parallel"),
        name="out_proj_a",
    )(y, m, x, g.reshape(1, D_MODEL), w)


def _out_b_body(o1_ref, o2_ref, o3_ref, l1_ref, l2_ref, l3_ref, m_ref, x_ref, g_ref, w_ref, o_ref):
    l1, l2, l3 = l1_ref[...], l2_ref[...], l3_ref[...]
    mx = jnp.maximum(jnp.maximum(l1, l2), l3)
    e1, e2, e3 = jnp.exp(l1 - mx), jnp.exp(l2 - mx), jnp.exp(l3 - mx)
    den = e1 + e2 + e3
    mix = (e1 / den) * o1_ref[...] + (e2 / den) * o2_ref[...] + (e3 / den) * o3_ref[...]
    acc = _dot(mix.astype(BF16), w_ref[0:DIL_GW, :])
    acc += _dot(m_ref[...].astype(BF16), w_ref[DIL_GW:DIL_GW + MEM_Q, :])
    o_ref[...] = x_ref[...] + _rms(acc, g_ref[...])


def _out_proj_b(outs, lses, m, x, g, w):
    t = x.shape[0]
    tm = _row_tile(t, 512)
    row = lambda width: pl.BlockSpec((tm, width), lambda i: (i, 0))
    return pl.pallas_call(
        _out_b_body,
        grid=(t // tm,),
        in_specs=[row(DIL_GW)] * 6 + [row(MEM_Q), row(D_MODEL), _const_spec((1, D_MODEL)),
                                      _const_spec((DIL_GW + MEM_Q, D_MODEL))],
        out_specs=row(D_MODEL),
        out_shape=jax.ShapeDtypeStruct((t, D_MODEL), F32),
        compiler_params=_params("parallel"),
        name="out_proj_b",
    )(*outs, *lses, m, x, g.reshape(1, D_MODEL), w)


def _softmax_rows(s):
    e = jnp.exp(s - jnp.max(s, axis=-1, keepdims=True))
    return e / jnp.sum(e, axis=-1, keepdims=True)


def _mem_prompt_body(q_ref, kv_ref, o_ref):
    for h in range(MEM_HEADS):
        lo = h * MEM_HD
        q = q_ref[:, lo:lo + MEM_HD].astype(BF16)
        k = kv_ref[:, lo:lo + MEM_HD].astype(BF16)
        v = kv_ref[:, MEM_Q + lo:MEM_Q + lo + MEM_HD].astype(BF16)
        p = _softmax_rows(_dot_nt(q, k) * (MEM_HD ** -0.5))
        o_ref[:, lo:lo + MEM_HD] = _dot(p.astype(BF16), v)


def _mem_attn_prompt(u, q_block, mem_kv, layer, b, s):
    c = u.shape[1]
    tq = _row_tile(s, 512)
    out = pl.pallas_call(
        _mem_prompt_body,
        grid=(b, s // tq),
        in_specs=[pl.BlockSpec((None, tq, MEM_Q), lambda i, j: (i, j, q_block)),
                  pl.BlockSpec((None, None, N_MEM, 2 * MEM_Q), lambda i, j: (layer, i, 0, 0))],
        out_specs=pl.BlockSpec((None, tq, MEM_Q), lambda i, j: (i, j, 0)),
        out_shape=jax.ShapeDtypeStruct((b, s, MEM_Q), F32),
        compiler_params=_params("parallel", "parallel"),
        name="mem_attn_prompt",
    )(u.reshape(b, s, c), mem_kv)
    return out.reshape(b * s, MEM_Q)


MEM_BT = 4
MEM_LT = MEM_HD // LANES
MEM_ROWS = 2 * MEM_LT * MEM_HEADS


def _mem_rows(kv_ref, bb, kv, h):
    parts = [kv_ref[bb, pl.ds((kv * MEM_LT + lt) * MEM_HEADS + h, N_MEM, stride=MEM_ROWS), :]
             for lt in range(MEM_LT)]
    return jnp.concatenate(parts, axis=1)


def _mem_sample_body(q_ref, kv_ref, o_ref, *, tlen):
    rows = MEM_BT * tlen
    owner = lax.broadcasted_iota(jnp.int32, (rows, 1), 0) // tlen
    q_all = q_ref[...]
    for h in range(MEM_HEADS):
        lo = h * MEM_HD
        q = q_all[:, lo:lo + MEM_HD].astype(BF16)
        s = jnp.zeros((rows, N_MEM), F32)
        for bb in range(MEM_BT):
            k = _mem_rows(kv_ref, bb, 0, h).astype(BF16)
            s = jnp.where(owner == bb, _dot_nt(q, k), s)
        p = _softmax_rows(s * (MEM_HD ** -0.5)).astype(BF16)
        o = jnp.zeros((rows, MEM_HD), F32)
        for bb in range(MEM_BT):
            v = _mem_rows(kv_ref, bb, 1, h).astype(BF16)
            o = jnp.where(owner == bb, _dot(p, v), o)
        o_ref[:, lo:lo + MEM_HD] = o


def _mem_attn_sample(u, q_block, mem_kv, layer, b, tlen):
    rows = MEM_BT * tlen
    depth = mem_kv.shape[0]
    mem_kv = mem_kv.reshape(depth, b, N_MEM, 2, MEM_HEADS, MEM_LT, LANES).transpose(0, 1, 2, 3, 5, 4, 6)
    mem_kv = mem_kv.reshape(depth, b, N_MEM * MEM_ROWS, LANES)
    return pl.pallas_call(
        functools.partial(_mem_sample_body, tlen=tlen),
        grid=(b // MEM_BT,),
        in_specs=[pl.BlockSpec((rows, MEM_Q), lambda i: (i, q_block)),
                  pl.BlockSpec((None, MEM_BT, N_MEM * MEM_ROWS, LANES), lambda i: (layer, i, 0, 0))],
        out_specs=pl.BlockSpec((rows, MEM_Q), lambda i: (i, 0)),
        out_shape=jax.ShapeDtypeStruct((b * tlen, MEM_Q), F32),
        compiler_params=_params("parallel"),
        name="mem_attn_sample",
    )(u, mem_kv)


DIL_TB = 2048


def _dil_prompt_body(sl_ref, q_ref, kp_ref, vp_ref, kc_ref, vc_ref, o_ref, l_ref, *, gi, dil):
    span = DIL_NK * dil
    has_prev = pl.program_id(2) > 0
    qi = lax.broadcasted_iota(jnp.int32, (DIL_NK, DIL_NK), 0)
    kj = lax.broadcasted_iota(jnp.int32, (DIL_NK, DIL_NK), 1)
    du_prev = DIL_NK + qi - kj
    du_own = qi - kj
    valid_prev = du_prev <= DIL_NK
    valid_own = du_own >= 0
    neg_slope = -sl_ref[gi * DIL_QH + pl.program_id(1)]
    bias_prev = neg_slope * (dil * du_prev).astype(F32)
    bias_own = neg_slope * (dil * du_own).astype(F32)
    scale = DIL_HD ** -0.5

    def rows(start):
        return pl.ds(start, DIL_NK) if dil == 1 else pl.ds(start, DIL_NK, stride=dil)

    def scores(blk, r):
        own = rows(blk * span + r)
        if blk == 0:
            k_prev, v_prev, ok_prev = kp_ref[rows(r), :], vp_ref[rows(r), :], valid_prev & has_prev
        else:
            prev = rows((blk - 1) * span + r)
            k_prev, v_prev, ok_prev = kc_ref[prev, :], vc_ref[prev, :], valid_prev
        q = q_ref[own, :].astype(BF16)
        s_prev = jnp.where(ok_prev, _dot_nt(q, k_prev.astype(BF16)) * scale + bias_prev, -jnp.inf)
        s_own = jnp.where(valid_own, _dot_nt(q, kc_ref[own, :].astype(BF16)) * scale + bias_own, -jnp.inf)
        mx = jnp.maximum(jnp.max(s_prev, axis=-1, keepdims=True), jnp.max(s_own, axis=-1, keepdims=True))
        return own, s_prev, s_own, mx, v_prev

    def weights(own, s_prev, s_own, mx, v_prev):
        e_prev = jnp.exp(s_prev - mx)
        e_own = jnp.exp(s_own - mx)
        den = jnp.sum(e_prev, axis=-1, keepdims=True) + jnp.sum(e_own, axis=-1, keepdims=True)
        return own, (e_prev / den).astype(BF16), (e_own / den).astype(BF16), mx + jnp.log(den), v_prev

    def finish(own, p_prev, p_own, lse, v_prev):
        o_ref[own, :] = _dot(p_prev, v_prev.astype(BF16)) + _dot(p_own, vc_ref[own, :].astype(BF16))
        l_ref[own, :] = jnp.broadcast_to(lse, (DIL_NK, DIL_HD))

    def streams(group):
        for st in [weights(*sc) for sc in [scores(blk, r) for blk, r in group]]:
            finish(*st)

    together = 4
    if dil <= together:
        todo = [(blk, r) for blk in range(DIL_TB // span) for r in range(dil)]
        for g0 in range(0, len(todo), together):
            streams(todo[g0:g0 + together])
    else:
        for blk in range(DIL_TB // span):
            def per_group(g, carry, blk=blk):
                streams([(blk, g * together + k) for k in range(together)])
                return carry

            lax.fori_loop(0, dil // together, per_group, 0)


def _dil_attn_prompt(slopes, u, kv, gi, b, s):
    window, dil = DIL_CONFIGS[gi]
    span = DIL_NK * dil
    assert window // dil == DIL_NK and s % DIL_TB == 0 and DIL_TB % span == 0
    per_tb = DIL_TB // span
    qpk = DIL_QH // DIL_KVH
    q0 = (MEM_Q + gi * DIL_GW) // DIL_HD
    k0 = gi * DIL_GW // DIL_HD
    v0 = k0 + DIL_KVH
    blk = lambda imap: pl.BlockSpec((None, DIL_TB, DIL_HD), imap)
    tail = lambda imap: pl.BlockSpec((None, span, DIL_HD), imap)
    u3 = u.reshape(b, s, U_B)
    kv3 = kv.reshape(b, s, KV_W)
    o, lse = pl.pallas_call(
        functools.partial(_dil_prompt_body, gi=gi, dil=dil),
        grid=(b, DIL_QH, s // DIL_TB),
        in_specs=[pl.BlockSpec(memory_space=pltpu.SMEM),
                  blk(lambda i, h, j: (i, j, q0 + h)),
                  tail(lambda i, h, j: (i, jnp.maximum(j * per_tb - 1, 0), k0 + h // qpk)),
                  tail(lambda i, h, j: (i, jnp.maximum(j * per_tb - 1, 0), v0 + h // qpk)),
                  blk(lambda i, h, j: (i, j, k0 + h // qpk)),
                  blk(lambda i, h, j: (i, j, v0 + h // qpk))],
        out_specs=[blk(lambda i, h, j: (i, j, h))] * 2,
        out_shape=[jax.ShapeDtypeStruct((b, s, DIL_GW), F32)] * 2,
        compiler_params=_params("parallel", "parallel", "arbitrary"),
        name=f"dil_attn_prompt_g{gi}",
    )(slopes, u3, kv3, kv3, kv3, kv3)
    return o.reshape(b * s, DIL_GW), lse.reshape(b * s, DIL_GW)


DIL_BT = 4


def _dil_sample_body(sl_ref, q_ref, buf_ref, new_ref, o_ref, l_ref, *, gi, dil, tlen):
    mi = lax.broadcasted_iota(jnp.int32, (DIL_NK, 1), 0)
    si = lax.broadcasted_iota(jnp.int32, (tlen, 1), 0)
    scale = DIL_HD ** -0.5
    per_pos = dil * 2 * DIL_KVH

    def per_batch(bb, carry):
        q_all = q_ref[bb]
        new = new_ref[bb]
        for t in range(tlen):
            if dil == 1:
                res = 0
                j_buf, ok_buf = DIL_NK + t - mi, mi >= t
                j_new, ok_new = t - si, si <= t
            else:
                res = t
                j_buf, ok_buf = DIL_NK - mi, mi >= 0
                j_new, ok_new = jnp.zeros_like(si), si == t
            dist_buf = (dil * j_buf).astype(F32)
            dist_new = (dil * j_new).astype(F32)
            for kvh in range(DIL_KVH):
                klo = kvh * DIL_HD
                vlo = DIL_KVH * DIL_HD + klo
                k_buf = buf_ref[bb, pl.ds((res * 2 + 0) * DIL_KVH + kvh, DIL_NK, stride=per_pos), :]
                v_buf = buf_ref[bb, pl.ds((res * 2 + 1) * DIL_KVH + kvh, DIL_NK, stride=per_pos), :]
                k_new = new[:, klo:klo + DIL_HD]
                v_new = new[:, vlo:vlo + DIL_HD]
                for qh in range(DIL_QH // DIL_KVH):
                    head = kvh * (DIL_QH // DIL_KVH) + qh
                    lo = head * DIL_HD
                    neg_slope = -sl_ref[gi * DIL_QH + head]
                    q = q_all[t:t + 1, lo:lo + DIL_HD]
                    s_buf = jnp.sum(q * k_buf, axis=-1, keepdims=True) * scale + neg_slope * dist_buf
                    s_new = jnp.sum(q * k_new, axis=-1, keepdims=True) * scale + neg_slope * dist_new
                    s_buf = jnp.where(ok_buf, s_buf, -jnp.inf)
                    s_new = jnp.where(ok_new, s_new, -jnp.inf)
                    mx = jnp.maximum(jnp.max(s_buf, axis=0, keepdims=True), jnp.max(s_new, axis=0, keepdims=True))
                    e_buf = jnp.exp(s_buf - mx)
                    e_new = jnp.exp(s_new - mx)
                    den = jnp.sum(e_buf, axis=0, keepdims=True) + jnp.sum(e_new, axis=0, keepdims=True)
                    acc = jnp.sum(e_buf * v_buf, axis=0, keepdims=True) + jnp.sum(e_new * v_new, axis=0, keepdims=True)
                    o_ref[bb, t:t + 1, lo:lo + DIL_HD] = acc / den
                    l_ref[bb, t:t + 1, lo:lo + DIL_HD] = jnp.broadcast_to(mx + jnp.log(den), (1, DIL_HD))
        return carry

    lax.fori_loop(0, DIL_BT, per_batch, 0)


def _dil_attn_sample(slopes, u, kv_new, win_buf, gi, b, tlen):
    window, dil = DIL_CONFIGS[gi]
    lb = win_buf.shape[1]
    assert window // dil == DIL_NK and lb == window and (dil == 1 or tlen <= dil)
    per_pos = dil * 2 * DIL_KVH
    buf = win_buf.reshape(b, DIL_NK * per_pos, DIL_HD)
    buf_spec = pl.BlockSpec((DIL_BT, DIL_NK * per_pos, DIL_HD), lambda i: (i, 0, 0))
    o, lse = pl.pallas_call(
        functools.partial(_dil_sample_body, gi=gi, dil=dil, tlen=tlen),
        grid=(b // DIL_BT,),
        in_specs=[pl.BlockSpec(memory_space=pltpu.SMEM),
                  pl.BlockSpec((DIL_BT, tlen, DIL_GW), lambda i: (i, 0, MEM_Q // DIL_GW + gi)),
                  buf_spec,
                  pl.BlockSpec((DIL_BT, tlen, DIL_GW), lambda i: (i, 0, gi))],
        out_specs=[pl.BlockSpec((DIL_BT, tlen, DIL_GW), lambda i: (i, 0, 0))] * 2,
        out_shape=[jax.ShapeDtypeStruct((b, tlen, DIL_GW), F32)] * 2,
        compiler_params=_params("parallel"),
        name=f"dil_attn_sample_g{gi}",
    )(slopes, u.reshape(b, tlen, U_B), buf, kv_new.reshape(b, tlen, KV_W))
    return o.reshape(b * tlen, DIL_GW), lse.reshape(b * tlen, DIL_GW)


def _pair_cols(a, pair, lane):
    h0 = 2 * pair
    return jnp.where(lane < SSD_HEADDIM, a[:, h0:h0 + 1], a[:, h0 + 1:h0 + 2])


X_TILES = D_INNER // LANES
B_TILE0 = X_TILES
C_TILE0 = X_TILES + BC_DIM // LANES
XBC_TILES = CONV_DIM // LANES


def _ssd_intra(xbc_ref, dt_raw, par_ref, tri_ref, seg_ref, y_ref, xddt_ref):
    dt_bias, a_log = par_ref[0:1, 0:LANES], par_ref[1:2, 0:LANES]
    pre = dt_raw + dt_bias
    dt = jnp.maximum(pre, 0.0) + jnp.log(1.0 + jnp.exp(-jnp.abs(pre)))
    d_a = dt * -jnp.exp(a_log)
    tri = tri_ref[...]
    acum = jnp.dot(tri, d_a, precision=HIGHEST, preferred_element_type=F32)
    a_end = jnp.dot(seg_ref[...], d_a, precision=HIGHEST, preferred_element_type=F32)
    acum_t = acum.T
    causal = tri > 0.5
    e_acum = jnp.exp(acum)
    e_end = jnp.exp(a_end)
    d_end = jnp.exp(a_end - acum)
    lane = lax.broadcasted_iota(jnp.int32, (CHUNK, LANES), 1)
    for grp in range(SSD_GROUPS):
        cb = _dot_nt(xbc_ref[C_TILE0 + grp].astype(BF16), xbc_ref[B_TILE0 + grp].astype(BF16))
        for half in range(2):
            pair = 2 * grp + half
            plo = pair * LANES
            xs = xbc_ref[pair]
            xd = xs * _pair_cols(dt, pair, lane)
            xd16 = xd.astype(BF16)
            yd = []
            for k in range(2):
                h = 2 * pair + k
                decay = jnp.exp(jnp.where(causal, acum[:, h:h + 1] - acum_t[h:h + 1, :], -jnp.inf))
                yd.append(_dot((cb * decay).astype(BF16), xd16))
            y_ref[:, plo:plo + LANES] = jnp.where(lane < SSD_HEADDIM, yd[0], yd[1]) + par_ref[2:3, plo:plo + LANES] * xs
            xddt_ref[plo:plo + LANES, :] = (xd * _pair_cols(d_end, pair, lane)).T
    return e_acum, e_end


def _ssd_finish(y_ref, z_ref, par_ref, o_ref):
    o_ref[...] = _rms(y_ref[...] * _silu(z_ref[...]), par_ref[3:4, :])


def _ssd_prompt_body(z_ref, x_ref, b_ref, c_ref, dt_ref, cw_ref, par_ref, tri_ref, seg_ref,
                     o_ref, st_ref, ext_ref, xbc_ref, y_ref, xddt_ref):
    first = pl.program_id(1) == 0

    @pl.when(first)
    def _():
        ext_ref[:, 0:8, :] = jnp.zeros((XBC_TILES, 8, LANES), F32)
        st_ref[...] = jnp.zeros_like(st_ref)

    for t in range(XBC_TILES):
        lo = t * LANES
        if t < B_TILE0:
            ext_ref[t, 8:8 + CHUNK, :] = x_ref[:, lo:lo + LANES]
        elif t < C_TILE0:
            ext_ref[t, 8:8 + CHUNK, :] = b_ref[:, lo - D_INNER:lo - D_INNER + LANES]
        else:
            ext_ref[t, 8:8 + CHUNK, :] = c_ref[:, lo - D_INNER - BC_DIM:lo - D_INNER - BC_DIM + LANES]

    for t in range(XBC_TILES):
        acc = cw_ref[t, CONV_W:CONV_W + 1, :] + ext_ref[t, 5:5 + CHUNK, :] * cw_ref[t, 0:1, :]
        for k in range(1, CONV_W):
            acc = acc + ext_ref[t, 5 + k:5 + k + CHUNK, :] * cw_ref[t, k:k + 1, :]
        xbc_ref[t] = _silu(acc)
    ext_ref[:, 0:8, :] = ext_ref[:, CHUNK:CHUNK + 8, :]

    e_acum, e_end = _ssd_intra(xbc_ref, dt_ref[...], par_ref, tri_ref, seg_ref, y_ref, xddt_ref)

    lane = lax.broadcasted_iota(jnp.int32, (CHUNK, LANES), 1)
    gw = SSD_STATE * 2
    for grp in range(SSD_GROUPS):
        rows = slice(grp * gw, (grp + 1) * gw)
        prev = st_ref[rows, :]
        y_off = _dot_nt(xbc_ref[C_TILE0 + grp].astype(BF16), prev.astype(BF16))
        for half in range(2):
            pair = 2 * grp + half
            plo = pair * LANES
            y_ref[:, plo:plo + LANES] += y_off[:, half * LANES:(half + 1) * LANES] * _pair_cols(e_acum, pair, lane)
        new = _dot(xddt_ref[rows, :].astype(BF16), xbc_ref[B_TILE0 + grp].astype(BF16))
        for k in range(4):
            h = 4 * grp + k
            hr = slice(grp * gw + k * SSD_HEADDIM, grp * gw + (k + 1) * SSD_HEADDIM)
            dec = jnp.broadcast_to(jnp.broadcast_to(e_end[:, h:h + 1], (CHUNK, SSD_STATE))[0:1, :],
                                   (SSD_HEADDIM, SSD_STATE))
            st_ref[hr, :] = prev[k * SSD_HEADDIM:(k + 1) * SSD_HEADDIM, :] * dec + new[k * SSD_HEADDIM:(k + 1) * SSD_HEADDIM, :]

    _ssd_finish(y_ref, z_ref, par_ref, o_ref)


def _ssd_consts(segment):
    r = jnp.arange(CHUNK)
    same = (r[:, None] // segment) == (r[None, :] // segment)
    return (same & (r[None, :] <= r[:, None])).astype(F32), same.astype(F32)


def _ssd_prompt(u, dt_raw, conv_wb, par, b, s):
    assert s % CHUNK == 0
    tri, seg = _ssd_consts(CHUNK)
    u3 = u.reshape(b, s, U_A)
    col = lambda width, blk: pl.BlockSpec((None, CHUNK, width), lambda i, c: (i, c, blk))
    y, st = pl.pallas_call(
        _ssd_prompt_body,
        grid=(b, s // CHUNK),
        in_specs=[col(D_INNER, 0), col(D_INNER, 1), col(BC_DIM, 2 * D_INNER // BC_DIM),
                  col(BC_DIM, 2 * D_INNER // BC_DIM + 1), col(LANES, 0),
                  _const_spec((XBC_TILES, 8, LANES)), _const_spec((4, D_INNER)),
                  _const_spec((CHUNK, CHUNK)), _const_spec((CHUNK, CHUNK))],
        out_specs=[pl.BlockSpec((None, CHUNK, D_INNER), lambda i, c: (i, c, 0)),
                   pl.BlockSpec((None, D_INNER, SSD_STATE), lambda i, c: (i, 0, 0))],
        out_shape=[jax.ShapeDtypeStruct((b, s, D_INNER), F32),
                   jax.ShapeDtypeStruct((b, D_INNER, SSD_STATE), F32)],
        scratch_shapes=[pltpu.VMEM((XBC_TILES, CHUNK + 8, LANES), F32), pltpu.VMEM((XBC_TILES, CHUNK, LANES), F32),
                        pltpu.VMEM((CHUNK, D_INNER), F32), pltpu.VMEM((D_INNER, CHUNK), F32)],
        compiler_params=_params("parallel", "arbitrary"),
        name="ssd_prompt",
    )(u3, u3, u3, u3, dt_raw.reshape(b, s, LANES), conv_wb, par, tri, seg)
    return y.reshape(b * s, D_INNER), st


def _conv_sample_body(seq_ref, cw_ref, o_ref, *, tlen):
    acc = cw_ref[4:5, :] + seq_ref[:, 0:tlen, :] * cw_ref[0:1, :]
    for k in range(1, CONV_W):
        acc = acc + seq_ref[:, k:k + tlen, :] * cw_ref[k:k + 1, :]
    o_ref[...] = _silu(acc)


def _conv_sample(seq, conv_wb, tlen):
    b = seq.shape[0]
    bt = 8
    return pl.pallas_call(
        functools.partial(_conv_sample_body, tlen=tlen),
        grid=(b // bt,),
        in_specs=[pl.BlockSpec((bt, CONV_W - 1 + tlen, CONV_DIM), lambda i: (i, 0, 0)),
                  _const_spec((8, CONV_DIM))],
        out_specs=pl.BlockSpec((bt, tlen, CONV_DIM), lambda i: (i, 0, 0)),
        out_shape=jax.ShapeDtypeStruct((b, tlen, CONV_DIM), F32),
        compiler_params=_params("parallel"),
        name="conv_sample",
    )(seq, conv_wb)


SSD_BT = 8


def _ssd_sample_body(z_ref, xbc_in_ref, dt_ref, par_ref, tri_ref, seg_ref, st_in_ref,
                     o_ref, st_out_ref, xbc_ref, dtp_ref, zp_ref, y_ref, xddt_ref, op_ref, *, tlen):
    rows = SSD_BT * tlen
    xbc_ref[...] = jnp.zeros_like(xbc_ref)
    dtp_ref[...] = jnp.zeros_like(dtp_ref)
    zp_ref[...] = jnp.zeros_like(zp_ref)
    for t in range(XBC_TILES):
        xbc_ref[t, 0:rows, :] = xbc_in_ref[:, t * LANES:(t + 1) * LANES]
    dtp_ref[0:rows, :] = dt_ref[...]
    zp_ref[0:rows, :] = z_ref[...]
    e_acum, e_end = _ssd_intra(xbc_ref, dtp_ref[...], par_ref, tri_ref, seg_ref, y_ref, xddt_ref)

    lane = lax.broadcasted_iota(jnp.int32, (CHUNK, LANES), 1)
    row_b = lax.broadcasted_iota(jnp.int32, (CHUNK, 1), 0) // tlen
    win = 16
    per_win = win // tlen
    win_b = lax.broadcasted_iota(jnp.int32, (win, 1), 0) // tlen
    gw = SSD_STATE * 2
    e_end_b = [jnp.broadcast_to(e_end[:, h:h + 1], (CHUNK, SSD_STATE)) for h in range(SSD_HEADS)]
    for grp in range(SSD_GROUPS):
        rsl = slice(grp * gw, (grp + 1) * gw)
        c16 = xbc_ref[C_TILE0 + grp].astype(BF16)
        b32 = xbc_ref[B_TILE0 + grp]
        xddt16 = xddt_ref[rsl, :].astype(BF16)
        for w in range(rows // win):
            c_win = c16[w * win:(w + 1) * win, :]
            y_off = jnp.zeros((win, gw), F32)
            for k in range(per_win):
                bb = w * per_win + k
                prev = st_in_ref[bb, rsl, :]
                y_off = jnp.where(win_b == k, _dot_nt(c_win, prev.astype(BF16)), y_off)
                new = _dot(xddt16, jnp.where(row_b == bb, b32, 0.0).astype(BF16))
                for hh in range(4):
                    h = 4 * grp + hh
                    hs = slice(hh * SSD_HEADDIM, (hh + 1) * SSD_HEADDIM)
                    dec = jnp.broadcast_to(e_end_b[h][bb * tlen:bb * tlen + 1, :], (SSD_HEADDIM, SSD_STATE))
                    st_out_ref[bb, grp * gw + hh * SSD_HEADDIM:grp * gw + (hh + 1) * SSD_HEADDIM, :] = (
                        prev[hs, :] * dec + new[hs, :])
            for half in range(2):
                pair = 2 * grp + half
                plo = pair * LANES
                scale = _pair_cols(e_acum, pair, lane)[w * win:(w + 1) * win, :]
                y_ref[w * win:(w + 1) * win, plo:plo + LANES] += y_off[:, half * LANES:(half + 1) * LANES] * scale

    _ssd_finish(y_ref, zp_ref, par_ref, op_ref)
    o_ref[...] = op_ref[0:rows, :]


def _ssd_sample(u, xbc, dt_raw, par, state, layer, b, tlen):
    rows = SSD_BT * tlen
    assert CHUNK % rows == 0 and 16 % tlen == 0 and rows % 16 == 0
    tri, seg = _ssd_consts(tlen)
    row = lambda width, blk: pl.BlockSpec((rows, width), lambda i: (i, blk))
    st_spec = pl.BlockSpec((SSD_BT, D_INNER, SSD_STATE), lambda i: (i, 0, 0))
    st_in_spec = pl.BlockSpec((None, SSD_BT, D_INNER, SSD_STATE), lambda i: (layer, i, 0, 0))
    y, st = pl.pallas_call(
        functools.partial(_ssd_sample_body, tlen=tlen),
        grid=(b // SSD_BT,),
        in_specs=[row(D_INNER, 0), row(CONV_DIM, 0), row(LANES, 0), _const_spec((4, D_INNER)),
                  _const_spec((CHUNK, CHUNK)), _const_spec((CHUNK, CHUNK)), st_in_spec],
        out_specs=[row(D_INNER, 0), st_spec],
        out_shape=[jax.ShapeDtypeStruct((b * tlen, D_INNER), F32),
                   jax.ShapeDtypeStruct((b, D_INNER, SSD_STATE), F32)],
        scratch_shapes=[pltpu.VMEM((XBC_TILES, CHUNK, LANES), F32), pltpu.VMEM((CHUNK, LANES), F32),
                        pltpu.VMEM((CHUNK, D_INNER), F32), pltpu.VMEM((CHUNK, D_INNER), F32),
                        pltpu.VMEM((D_INNER, CHUNK), F32), pltpu.VMEM((CHUNK, D_INNER), F32)],
        compiler_params=_params("parallel"),
        name="ssd_sample",
    )(u, xbc, dt_raw, par, tri, seg, state)
    return y, st


def _prep_weights(norm_g, w_ffn_gu, w_ffn_down, w_in_a, conv_w, conv_b, dt_bias, a_log, d_skip, ssd_norm_g,
                  w_out_a, w_in_b, w_out_b, kv_norm_g, w_kv_shared, mem_norm_g, w_mem_kv, final_norm_g):
    p = {}
    p["g_ffn"] = [[jnp.stack([norm_g[l, 0 + 4 * k], norm_g[l, 1 + 4 * k], final_norm_g]) for k in range(2)]
                  for l in range(DEPTH)]
    p["w_gu"] = w_ffn_gu.astype(BF16)
    p["w_down"] = w_ffn_down.astype(BF16)
    p["w_in_a"] = jnp.concatenate([w_in_a[:, :, :SSD_MAIN], w_in_a[:, :, SSD_IN:]], axis=-1).astype(BF16)
    p["w_dt"] = jnp.pad(w_in_a[:, :, SSD_MAIN:SSD_IN], ((0, 0), (0, 0), (0, LANES - SSD_HEADS))).astype(BF16)
    p["conv_wb"] = jnp.concatenate([conv_w, conv_b[:, None, :], jnp.zeros((N_A, 3, CONV_DIM), F32)], axis=1)
    p["conv_wb_tiles"] = p["conv_wb"].reshape(N_A, 8, XBC_TILES, LANES).transpose(0, 2, 1, 3)
    lane_pad = lambda v: jnp.pad(v, ((0, 0), (0, D_INNER - v.shape[1])))
    p["ssd_par"] = jnp.stack([lane_pad(dt_bias), lane_pad(a_log), jnp.repeat(d_skip, SSD_HEADDIM, axis=1),
                              ssd_norm_g], axis=1)
    p["w_out_a"] = w_out_a.astype(BF16)
    q_dil = N_DIL * DIL_GW
    p["w_in_b"] = jnp.concatenate([w_in_b[:, :, q_dil:], w_in_b[:, :, :q_dil]], axis=-1).astype(BF16)
    p["w_out_b"] = w_out_b.astype(BF16)
    p["w_kv"] = w_kv_shared.astype(BF16)
    p["w_mem_kv"] = w_mem_kv.astype(BF16)
    n = N_DIL * DIL_QH
    p["slopes"] = jnp.exp2(-8.0 * jnp.arange(1, n + 1, dtype=F32) / n)
    p["norm_g"] = norm_g
    p["kv_norm_g"] = kv_norm_g
    p["mem_norm_g"] = mem_norm_g
    return p


def _trunk(p, x, b, s, mem_kv, conv_prev, ssm_prev, win_bufs):
    prompt = win_bufs is None
    ssm_new, conv_new = [], []
    kv = None
    for l in range(DEPTH):
        g = p["norm_g"][l]
        x = _ffn(x, p["g_ffn"][l][0], p["w_gu"][l, 0], p["w_down"][l, 0])
        if l < N_A:
            u = _norm_matmul(x, g[2], p["w_in_a"][l])
            dt_raw = _norm_matmul(x, g[2], p["w_dt"][l])
            u3 = u.reshape(b, s, U_A)
            if prompt:
                y_mix, st = _ssd_prompt(u, dt_raw, p["conv_wb_tiles"][l], p["ssd_par"][l], b, s)
                conv_new.append(u3[:, s - (CONV_W - 1):, D_INNER:SSD_MAIN])
                mem_o = _mem_attn_prompt(u, SSD_MAIN // MEM_Q, mem_kv, l, b, s)
            else:
                seq = jnp.concatenate([conv_prev[l], u3[:, :, D_INNER:SSD_MAIN]], axis=1)
                xbc = _conv_sample(seq, p["conv_wb"][l], s).reshape(b * s, CONV_DIM)
                y_mix, st = _ssd_sample(u, xbc, dt_raw, p["ssd_par"][l], ssm_prev, l, b, s)
                conv_new.append(seq[:, s:])
                mem_o = _mem_attn_sample(u, SSD_MAIN // MEM_Q, mem_kv, l, b, s)
            ssm_new.append(st.reshape(b, SSD_HEADS, SSD_HEADDIM, SSD_STATE))
            x = _out_proj_a(y_mix, mem_o, x, g[3], p["w_out_a"][l])
        else:
            i = l - N_A
            u = _norm_matmul(x, g[2], p["w_in_b"][i])
            outs, lses = [], []
            for gi in range(N_DIL):
                if prompt:
                    o, lse = _dil_attn_prompt(p["slopes"], u, kv, gi, b, s)
                else:
                    o, lse = _dil_attn_sample(p["slopes"], u, kv, win_bufs[gi], gi, b, s)
                outs.append(o)
                lses.append(lse)
            if prompt:
                mem_o = _mem_attn_prompt(u, 0, mem_kv, l, b, s)
            else:
                mem_o = _mem_attn_sample(u, 0, mem_kv, l, b, s)
            x = _out_proj_b(outs, lses, mem_o, x, g[3], p["w_out_b"][i])
        x = _ffn(x, p["g_ffn"][l][1], p["w_gu"][l, 1], p["w_down"][l, 1], final=(l == DEPTH - 1))
        if l == N_A - 1:
            kv = _norm_matmul(x, p["kv_norm_g"], p["w_kv"])
    return x, jnp.stack(conv_new), jnp.stack(ssm_new), kv


def kernel(x_prompt, x_sample, mem_prompt, cache_mem_kv, state_ssm, state_conv, cache_win_g1, cache_win_g2,
           cache_win_g3, norm_g, w_ffn_gu, w_ffn_down, w_in_a, conv_w, conv_b, dt_bias, a_log, d_skip,
           ssd_norm_g, w_out_a, w_in_b, w_out_b, kv_norm_g, w_kv_shared, mem_norm_g, w_mem_kv, final_norm_g):
    p = _prep_weights(norm_g, w_ffn_gu, w_ffn_down, w_in_a, conv_w, conv_b, dt_bias, a_log, d_skip, ssd_norm_g,
                      w_out_a, w_in_b, w_out_b, kv_norm_g, w_kv_shared, mem_norm_g, w_mem_kv, final_norm_g)
    bp, sp = x_prompt.shape[:2]
    bs, ss = x_sample.shape[:2]

    mem_flat = mem_prompt.reshape(bp * N_MEM, D_MODEL)
    mem_kv_p = jnp.stack([_norm_matmul(mem_flat, mem_norm_g[l], p["w_mem_kv"][l]).reshape(bp, N_MEM, 2 * MEM_Q)
                          for l in range(DEPTH)])
    y_p, conv_p, ssm_p, kv_p = _trunk(p, x_prompt.reshape(bp * sp, D_MODEL), bp, sp, mem_kv_p, None, None, None)
    kv_p = kv_p.reshape(bp, sp, KV_W)
    win_p = [kv_p[:, sp - min(w, sp):, gi * DIL_GW:(gi + 1) * DIL_GW].reshape(bp, min(w, sp), 2, DIL_KVH, DIL_HD)
             for gi, (w, _) in enumerate(DIL_CONFIGS)]

    y_s, conv_s, ssm_s, kv_s = _trunk(p, x_sample.reshape(bs * ss, D_MODEL), bs, ss, cache_mem_kv, state_conv,
                                      state_ssm.reshape(N_A, bs, D_INNER, SSD_STATE),
                                      (cache_win_g1, cache_win_g2, cache_win_g3))
    kv_s = kv_s.reshape(bs, ss, N_DIL, 2, DIL_KVH, DIL_HD)
    return (y_p.reshape(bp, sp, D_MODEL), y_s.reshape(bs, ss, D_MODEL),
            mem_kv_p.reshape(DEPTH, bp, N_MEM, 2, MEM_HEADS, MEM_HD), ssm_p, conv_p, win_p[0], win_p[1], win_p[2],
            ssm_s, conv_s, kv_s[:, :, 0], kv_s[:, :, 1], kv_s[:, :, 2])
```

```python
import functools

import jax
import jax.numpy as jnp
from jax import lax
from jax.experimental import pallas as pl
from jax.experimental.pallas import tpu as pltpu

F32 = jnp.float32
BF16 = jnp.bfloat16
HIGHEST = lax.Precision.HIGHEST

D_MODEL = 1024
DEPTH = 4
N_A = 2
D_FF = 2816
D_INNER = 2048
SSD_HEADDIM = 64
SSD_HEADS = 32
SSD_GROUPS = 8
SSD_STATE = 128
CONV_W = 4
CONV_DIM = D_INNER + 2 * SSD_GROUPS * SSD_STATE
BC_DIM = SSD_GROUPS * SSD_STATE
DIL_CONFIGS = ((128, 1), (512, 4), (2048, 16))
N_DIL = 3
DIL_QH = 4
DIL_KVH = 2
DIL_HD = 128
DIL_NK = 128
DIL_GW = DIL_QH * DIL_HD
N_MEM = 256
MEM_HEADS = 4
MEM_HD = 256
MEM_Q = MEM_HEADS * MEM_HD
EPS = 1e-6
SSD_MAIN = 2 * D_INNER + 2 * BC_DIM
SSD_IN = SSD_MAIN + SSD_HEADS
U_A = SSD_MAIN + MEM_Q
U_B = MEM_Q + N_DIL * DIL_GW
KV_W = N_DIL * 2 * DIL_KVH * DIL_HD

LANES = 128
CHUNK = 128
FF_CHUNK = 256
VMEM_LIMIT = 56 * 2 ** 20


def _params(*sem):
    return pltpu.CompilerParams(dimension_semantics=sem, vmem_limit_bytes=VMEM_LIMIT)


def _const_spec(shape):
    return pl.BlockSpec(shape, lambda *_: (0,) * len(shape), pipeline_mode=pl.Buffered(1))


def _rms(x, g):
    return x * lax.rsqrt(jnp.mean(x * x, axis=-1, keepdims=True) + EPS) * g


def _silu(x):
    return x / (1.0 + jnp.exp(-x))


def _dot(a, b):
    return jnp.dot(a, b, preferred_element_type=F32)


def _dot_nt(a, b):
    return lax.dot_general(a, b, (((1,), (1,)), ((), ())), preferred_element_type=F32)


def _row_tile(t, pref):
    return pref if t % pref == 0 else t


def _ffn_body(x_ref, g_ref, wgu_ref, wd_ref, o_ref, acc_ref, *, final):
    x = x_ref[...]
    h = _rms(x, g_ref[0:1, :]).astype(BF16)
    for c in range(D_FF // FF_CHUNK):
        lo = c * FF_CHUNK
        gate = _dot(h, wgu_ref[:, lo:lo + FF_CHUNK])
        up = _dot(h, wgu_ref[:, D_FF + lo:D_FF + lo + FF_CHUNK])
        part = _dot((_silu(gate) * up).astype(BF16), wd_ref[lo:lo + FF_CHUNK, :])
        if c == 0:
            acc_ref[...] = part
        else:
            acc_ref[...] += part
    y = x + 0.5 * _rms(acc_ref[...], g_ref[1:2, :])
    if final:
        y = _rms(y, g_ref[2:3, :])
    o_ref[...] = y


def _ffn(x, g3, wgu, wd, final=False):
    t = x.shape[0]
    tm = _row_tile(t, 512)
    return pl.pallas_call(
        functools.partial(_ffn_body, final=final),
        grid=(t // tm,),
        in_specs=[pl.BlockSpec((tm, D_MODEL), lambda i: (i, 0)),
                  _const_spec((3, D_MODEL)),
                  _const_spec((D_MODEL, 2 * D_FF)),
                  _const_spec((D_FF, D_MODEL))],
        out_specs=pl.BlockSpec((tm, D_MODEL), lambda i: (i, 0)),
        out_shape=jax.ShapeDtypeStruct((t, D_MODEL), F32),
        scratch_shapes=[pltpu.VMEM((tm, D_MODEL), F32)],
        compiler_params=_params("parallel"),
        name="ffn",
    )(x, g3, wgu, wd)


def _nmm_body(x_ref, g_ref, w_ref, o_ref, h_ref):
    @pl.when(pl.program_id(1) == 0)
    def _():
        h_ref[...] = _rms(x_ref[...], g_ref[...]).astype(BF16)

    o_ref[...] = _dot(h_ref[...], w_ref[...])


def _norm_matmul(x, g, w):
    t, n = x.shape[0], w.shape[1]
    tm = _row_tile(t, 1024)
    tn = n if n <= 2560 else (1024 if n % 1024 == 0 else 512)
    return pl.pallas_call(
        _nmm_body,
        grid=(t // tm, n // tn),
        in_specs=[pl.BlockSpec((tm, D_MODEL), lambda i, j: (i, 0)),
                  pl.BlockSpec((1, D_MODEL), lambda i, j: (0, 0)),
                  pl.BlockSpec((D_MODEL, tn), lambda i, j: (0, j))],
        out_specs=pl.BlockSpec((tm, tn), lambda i, j: (i, j)),
        out_shape=jax.ShapeDtypeStruct((t, n), F32),
        scratch_shapes=[pltpu.VMEM((tm, D_MODEL), BF16)],
        compiler_params=_params("parallel", "arbitrary"),
        name="norm_matmul",
    )(x, g.reshape(1, D_MODEL), w)


def _out_a_body(y_ref, m_ref, x_ref, g_ref, w_ref, o_ref):
    acc = _dot(y_ref[...], w_ref[0:D_INNER, :])
    acc += _dot(m_ref[...], w_ref[D_INNER:D_INNER + MEM_Q, :])
    o_ref[...] = x_ref[...] + _rms(acc, g_ref[...])


def _out_proj_a(y, m, x, g, w):
    t = x.shape[0]
    tm = _row_tile(t, 512)
    row = lambda width: pl.BlockSpec((tm, width), lambda i: (i, 0))
    return pl.pallas_call(
        _out_a_body,
        grid=(t // tm,),
        in_specs=[row(D_INNER), row(MEM_Q), row(D_MODEL), _const_spec((1, D_MODEL)),
                  _const_spec((D_INNER + MEM_Q, D_MODEL))],
        out_specs=row(D_MODEL),
        out_shape=jax.ShapeDtypeStruct((t, D_MODEL), F32),
        compiler_params=_params("parallel"),
        name="out_proj_a",
    )(y, m, x, g.reshape(1, D_MODEL), w)


def _out_b_body(o1_ref, o2_ref, o3_ref, l1_ref, l2_ref, l3_ref, m_ref, x_ref, g_ref, w_ref, o_ref):
    l1, l2, l3 = l1_ref[...], l2_ref[...], l3_ref[...]
    mx = jnp.maximum(jnp.maximum(l1, l2), l3)
    e1, e2, e3 = jnp.exp(l1 - mx), jnp.exp(l2 - mx), jnp.exp(l3 - mx)
    den = e1 + e2 + e3
    a1, a2, a3 = e1 / den, e2 / den, e3 / den
    mix = []
    for h in range(DIL_QH):
        c = slice(h * DIL_HD, (h + 1) * DIL_HD)
        mix.append(a1[:, h:h + 1] * o1_ref[:, c] + a2[:, h:h + 1] * o2_ref[:, c] + a3[:, h:h + 1] * o3_ref[:, c])
    acc = _dot(jnp.concatenate(mix, axis=1).astype(BF16), w_ref[0:DIL_GW, :])
    acc += _dot(m_ref[...], w_ref[DIL_GW:DIL_GW + MEM_Q, :])
    o_ref[...] = x_ref[...] + _rms(acc, g_ref[...])


def _out_proj_b(outs, lses, m, x, g, w):
    t = x.shape[0]
    tm = _row_tile(t, 512)
    row = lambda width: pl.BlockSpec((tm, width), lambda i: (i, 0))
    return pl.pallas_call(
        _out_b_body,
        grid=(t // tm,),
        in_specs=[row(DIL_GW)] * 3 + [row(LANES)] * 3 + [row(MEM_Q), row(D_MODEL), _const_spec((1, D_MODEL)),
                                                         _const_spec((DIL_GW + MEM_Q, D_MODEL))],
        out_specs=row(D_MODEL),
        out_shape=jax.ShapeDtypeStruct((t, D_MODEL), F32),
        compiler_params=_params("parallel"),
        name="out_proj_b",
    )(*outs, *lses, m, x, g.reshape(1, D_MODEL), w)


def _softmax_rows(s):
    e = jnp.exp(s - jnp.max(s, axis=-1, keepdims=True))
    return e / jnp.sum(e, axis=-1, keepdims=True)


def _mem_prompt_body(q_ref, kv_ref, o_ref):
    cols = [slice(h * MEM_HD, (h + 1) * MEM_HD) for h in range(MEM_HEADS)]
    scores = [_dot_nt(q_ref[:, c].astype(BF16), kv_ref[:, c].astype(BF16)) * (MEM_HD ** -0.5) for c in cols]
    probs = [_softmax_rows(s).astype(BF16) for s in scores]
    for h, c in enumerate(cols):
        v = kv_ref[:, MEM_Q + h * MEM_HD:MEM_Q + (h + 1) * MEM_HD].astype(BF16)
        o_ref[:, c] = _dot(probs[h], v).astype(o_ref.dtype)


def _mem_attn_prompt(u, q_block, mem_kv, layer, b, s):
    c = u.shape[1]
    tq = _row_tile(s, 1024)
    out = pl.pallas_call(
        _mem_prompt_body,
        grid=(b, s // tq),
        in_specs=[pl.BlockSpec((None, tq, MEM_Q), lambda i, j: (i, j, q_block)),
                  pl.BlockSpec((None, None, N_MEM, 2 * MEM_Q), lambda i, j: (layer, i, 0, 0))],
        out_specs=pl.BlockSpec((None, tq, MEM_Q), lambda i, j: (i, j, 0)),
        out_shape=jax.ShapeDtypeStruct((b, s, MEM_Q), BF16),
        compiler_params=_params("parallel", "parallel"),
        name="mem_attn_prompt",
    )(u.reshape(b, s, c), mem_kv)
    return out.reshape(b * s, MEM_Q)


MEM_BT = 4
MEM_LT = MEM_HD // LANES
MEM_ROWS = 2 * MEM_LT * MEM_HEADS


def _mem_rows(kv_ref, bb, kv, h):
    parts = [kv_ref[bb, pl.ds((kv * MEM_LT + lt) * MEM_HEADS + h, N_MEM, stride=MEM_ROWS), :]
             for lt in range(MEM_LT)]
    return jnp.concatenate(parts, axis=1)


def _mem_sample_body(q_ref, kv_ref, o_ref, *, tlen):
    rows = MEM_BT * tlen
    owner = lax.broadcasted_iota(jnp.int32, (rows, 1), 0) // tlen
    q_all = q_ref[...]
    for h in range(MEM_HEADS):
        lo = h * MEM_HD
        q = q_all[:, lo:lo + MEM_HD].astype(BF16)
        s = jnp.zeros((rows, N_MEM), F32)
        for bb in range(MEM_BT):
            k = _mem_rows(kv_ref, bb, 0, h).astype(BF16)
            s = jnp.where(owner == bb, _dot_nt(q, k), s)
        p = _softmax_rows(s * (MEM_HD ** -0.5)).astype(BF16)
        o = jnp.zeros((rows, MEM_HD), F32)
        for bb in range(MEM_BT):
            v = _mem_rows(kv_ref, bb, 1, h).astype(BF16)
            o = jnp.where(owner == bb, _dot(p, v), o)
        o_ref[:, lo:lo + MEM_HD] = o.astype(o_ref.dtype)


def _mem_attn_sample(u, q_block, mem_kv, layer, b, tlen):
    rows = MEM_BT * tlen
    depth = mem_kv.shape[0]
    mem_kv = mem_kv.reshape(depth, b, N_MEM, 2, MEM_HEADS, MEM_LT, LANES).transpose(0, 1, 2, 3, 5, 4, 6)
    mem_kv = mem_kv.reshape(depth, b, N_MEM * MEM_ROWS, LANES)
    return pl.pallas_call(
        functools.partial(_mem_sample_body, tlen=tlen),
        grid=(b // MEM_BT,),
        in_specs=[pl.BlockSpec((rows, MEM_Q), lambda i: (i, q_block)),
                  pl.BlockSpec((None, MEM_BT, N_MEM * MEM_ROWS, LANES), lambda i: (layer, i, 0, 0))],
        out_specs=pl.BlockSpec((rows, MEM_Q), lambda i: (i, 0)),
        out_shape=jax.ShapeDtypeStruct((b * tlen, MEM_Q), BF16),
        compiler_params=_params("parallel"),
        name="mem_attn_sample",
    )(u, mem_kv)


DIL_TB = 2048


def _dil_prompt_body(sl_ref, q_ref, kp_ref, vp_ref, kc_ref, vc_ref, o_ref, l_ref, *, gi, dil):
    span = DIL_NK * dil
    has_prev = pl.program_id(1) > 0
    head = pl.program_id(2)
    qi = lax.broadcasted_iota(jnp.int32, (DIL_NK, DIL_NK), 0)
    kj = lax.broadcasted_iota(jnp.int32, (DIL_NK, DIL_NK), 1)
    du_prev = DIL_NK + qi - kj
    du_own = qi - kj
    valid_prev = du_prev <= DIL_NK
    valid_own = du_own >= 0
    my_lane = kj == head
    neg_slope = -sl_ref[gi * DIL_QH + head]

    @pl.when(head == 0)
    def _():
        l_ref[...] = jnp.zeros_like(l_ref)

    bias_prev = neg_slope * (dil * du_prev).astype(F32)
    bias_own = neg_slope * (dil * du_own).astype(F32)
    scale = DIL_HD ** -0.5

    def rows(start):
        return pl.ds(start, DIL_NK) if dil == 1 else pl.ds(start, DIL_NK, stride=dil)

    def scores(blk, r):
        own = rows(blk * span + r)
        if blk == 0:
            k_prev, v_prev, ok_prev = kp_ref[rows(r), :], vp_ref[rows(r), :], valid_prev & has_prev
        else:
            prev = rows((blk - 1) * span + r)
            k_prev, v_prev, ok_prev = kc_ref[prev, :], vc_ref[prev, :], valid_prev
        q = q_ref[own, :].astype(BF16)
        s_prev = jnp.where(ok_prev, _dot_nt(q, k_prev.astype(BF16)) * scale + bias_prev, -jnp.inf)
        s_own = jnp.where(valid_own, _dot_nt(q, kc_ref[own, :].astype(BF16)) * scale + bias_own, -jnp.inf)
        mx = jnp.maximum(jnp.max(s_prev, axis=-1, keepdims=True), jnp.max(s_own, axis=-1, keepdims=True))
        return own, s_prev, s_own, mx, v_prev

    def weights(own, s_prev, s_own, mx, v_prev):
        e_prev = jnp.exp(s_prev - mx)
        e_own = jnp.exp(s_own - mx)
        den = jnp.sum(e_prev, axis=-1, keepdims=True) + jnp.sum(e_own, axis=-1, keepdims=True)
        return own, (e_prev / den).astype(BF16), (e_own / den).astype(BF16), mx + jnp.log(den), v_prev

    def finish(own, p_prev, p_own, lse, v_prev):
        o_ref[own, :] = _dot(p_prev, v_prev.astype(BF16)) + _dot(p_own, vc_ref[own, :].astype(BF16))
        l_ref[own, :] = jnp.where(my_lane, lse, l_ref[own, :])

    def streams(group):
        for st in [weights(*sc) for sc in [scores(blk, r) for blk, r in group]]:
            finish(*st)

    together = 8
    if dil <= together:
        todo = [(blk, r) for blk in range(DIL_TB // span) for r in range(dil)]
        for g0 in range(0, len(todo), together):
            streams(todo[g0:g0 + together])
    else:
        for blk in range(DIL_TB // span):
            def per_group(g, carry, blk=blk):
                streams([(blk, g * together + k) for k in range(together)])
                return carry

            lax.fori_loop(0, dil // together, per_group, 0)


def _dil_attn_prompt(slopes, u, kv, gi, b, s):
    window, dil = DIL_CONFIGS[gi]
    span = DIL_NK * dil
    assert window // dil == DIL_NK and s % DIL_TB == 0 and DIL_TB % span == 0
    per_tb = DIL_TB // span
    qpk = DIL_QH // DIL_KVH
    q0 = (MEM_Q + gi * DIL_GW) // DIL_HD
    k0 = gi * DIL_GW // DIL_HD
    v0 = k0 + DIL_KVH
    blk = lambda imap: pl.BlockSpec((None, DIL_TB, DIL_HD), imap)
    tail = lambda imap: pl.BlockSpec((None, span, DIL_HD), imap)
    u3 = u.reshape(b, s, U_B)
    kv3 = kv.reshape(b, s, KV_W)
    o, lse = pl.pallas_call(
        functools.partial(_dil_prompt_body, gi=gi, dil=dil),
        grid=(b, s // DIL_TB, DIL_QH),
        in_specs=[pl.BlockSpec(memory_space=pltpu.SMEM),
                  blk(lambda i, j, h: (i, j, q0 + h)),
                  tail(lambda i, j, h: (i, jnp.maximum(j * per_tb - 1, 0), k0 + h // qpk)),
                  tail(lambda i, j, h: (i, jnp.maximum(j * per_tb - 1, 0), v0 + h // qpk)),
                  blk(lambda i, j, h: (i, j, k0 + h // qpk)),
                  blk(lambda i, j, h: (i, j, v0 + h // qpk))],
        out_specs=[blk(lambda i, j, h: (i, j, h)),
                   pl.BlockSpec((None, DIL_TB, LANES), lambda i, j, h: (i, j, 0))],
        out_shape=[jax.ShapeDtypeStruct((b, s, DIL_GW), F32), jax.ShapeDtypeStruct((b, s, LANES), F32)],
        compiler_params=_params("parallel", "parallel", "arbitrary"),
        name=f"dil_attn_prompt_g{gi}",
    )(slopes, u3, kv3, kv3, kv3, kv3)
    return o.reshape(b * s, DIL_GW), lse.reshape(b * s, LANES)


DIL_BT = 4


def _dil_sample_body(sl_ref, q_ref, buf_ref, new_ref, o_ref, l_ref, *, gi, dil, tlen):
    mi = lax.broadcasted_iota(jnp.int32, (DIL_NK, 1), 0)
    si = lax.broadcasted_iota(jnp.int32, (tlen, 1), 0)
    head_lane = lax.broadcasted_iota(jnp.int32, (1, LANES), 1)
    scale = DIL_HD ** -0.5
    if len(buf_ref.shape) == 4:
        per_pos = buf_ref.shape[2]
        buf_ref = buf_ref.reshape(DIL_BT, DIL_NK * per_pos, DIL_HD)
    else:
        per_pos = buf_ref.shape[1] // DIL_NK

    def per_batch(bb, carry):
        q_all = q_ref[bb]
        new = new_ref[bb]
        for t in range(tlen):
            if dil == 1:
                res = 0
                j_buf, ok_buf = DIL_NK + t - mi, mi >= t
                j_new, ok_new = t - si, si <= t
            else:
                res = t
                j_buf, ok_buf = DIL_NK - mi, mi >= 0
                j_new, ok_new = jnp.zeros_like(si), si == t
            dist_buf = (dil * j_buf).astype(F32)
            dist_new = (dil * j_new).astype(F32)
            lse_row = jnp.zeros((1, LANES), F32)
            for kvh in range(DIL_KVH):
                klo = kvh * DIL_HD
                vlo = DIL_KVH * DIL_HD + klo
                k_buf = buf_ref[bb, pl.ds((res * 2 + 0) * DIL_KVH + kvh, DIL_NK, stride=per_pos), :]
                v_buf = buf_ref[bb, pl.ds((res * 2 + 1) * DIL_KVH + kvh, DIL_NK, stride=per_pos), :]
                k_new = new[:, klo:klo + DIL_HD]
                v_new = new[:, vlo:vlo + DIL_HD]
                for qh in range(DIL_QH // DIL_KVH):
                    head = kvh * (DIL_QH // DIL_KVH) + qh
                    lo = head * DIL_HD
                    neg_slope = -sl_ref[gi * DIL_QH + head]
                    q = q_all[t:t + 1, lo:lo + DIL_HD]
                    s_buf = jnp.sum(q * k_buf, axis=-1, keepdims=True) * scale + neg_slope * dist_buf
                    s_new = jnp.sum(q * k_new, axis=-1, keepdims=True) * scale + neg_slope * dist_new
                    s_buf = jnp.where(ok_buf, s_buf, -jnp.inf)
                    s_new = jnp.where(ok_new, s_new, -jnp.inf)
                    mx = jnp.maximum(jnp.max(s_buf, axis=0, keepdims=True), jnp.max(s_new, axis=0, keepdims=True))
                    e_buf = jnp.exp(s_buf - mx)
                    e_new = jnp.exp(s_new - mx)
                    den = jnp.sum(e_buf, axis=0, keepdims=True) + jnp.sum(e_new, axis=0, keepdims=True)
                    acc = jnp.sum(e_buf * v_buf, axis=0, keepdims=True) + jnp.sum(e_new * v_new, axis=0, keepdims=True)
                    o_ref[bb, t:t + 1, lo:lo + DIL_HD] = acc / den
                    lse_row = jnp.where(head_lane == head, mx + jnp.log(den), lse_row)
            l_ref[bb, t:t + 1, :] = lse_row
        return carry

    lax.fori_loop(0, DIL_BT, per_batch, 0)


def _dil_attn_sample(slopes, u, kv_new, win_buf, gi, b, tlen):
    window, dil = DIL_CONFIGS[gi]
    lb = win_buf.shape[1]
    assert window // dil == DIL_NK and lb == window and (dil == 1 or tlen <= dil)
    per_res = 2 * DIL_KVH
    fetch = 1 if dil == 1 else tlen
    if fetch == dil:
        buf = win_buf.reshape(b, DIL_NK * dil * per_res, DIL_HD)
        buf_spec = pl.BlockSpec((DIL_BT, DIL_NK * dil * per_res, DIL_HD), lambda i: (i, 0, 0))
    else:
        assert (fetch * per_res) % 8 == 0
        buf = win_buf.reshape(b, DIL_NK, dil * per_res, DIL_HD)
        buf_spec = pl.BlockSpec((DIL_BT, DIL_NK, fetch * per_res, DIL_HD), lambda i: (i, 0, 0, 0))
    o, lse = pl.pallas_call(
        functools.partial(_dil_sample_body, gi=gi, dil=dil, tlen=tlen),
        grid=(b // DIL_BT,),
        in_specs=[pl.BlockSpec(memory_space=pltpu.SMEM),
                  pl.BlockSpec((DIL_BT, tlen, DIL_GW), lambda i: (i, 0, MEM_Q // DIL_GW + gi)),
                  buf_spec,
                  pl.BlockSpec((DIL_BT, tlen, DIL_GW), lambda i: (i, 0, gi))],
        out_specs=[pl.BlockSpec((DIL_BT, tlen, DIL_GW), lambda i: (i, 0, 0)),
                   pl.BlockSpec((DIL_BT, tlen, LANES), lambda i: (i, 0, 0))],
        out_shape=[jax.ShapeDtypeStruct((b, tlen, DIL_GW), F32), jax.ShapeDtypeStruct((b, tlen, LANES), F32)],
        compiler_params=_params("parallel"),
        name=f"dil_attn_sample_g{gi}",
    )(slopes, u.reshape(b, tlen, U_B), buf, kv_new.reshape(b, tlen, KV_W))
    return o.reshape(b * tlen, DIL_GW), lse.reshape(b * tlen, LANES)


def _pair_cols(a, pair, lane):
    h0 = 2 * pair
    return jnp.where(lane < SSD_HEADDIM, a[:, h0:h0 + 1], a[:, h0 + 1:h0 + 2])


X_TILES = D_INNER // LANES
B_TILE0 = X_TILES
C_TILE0 = X_TILES + BC_DIM // LANES
XBC_TILES = CONV_DIM // LANES


def _ssd_intra(xbc_ref, dt_raw, par_ref, tri_ref, seg_ref, y_ref, xddt_ref):
    dt_bias, a_log = par_ref[0:1, 0:LANES], par_ref[1:2, 0:LANES]
    pre = dt_raw + dt_bias
    dt = jnp.maximum(pre, 0.0) + jnp.log(1.0 + jnp.exp(-jnp.abs(pre)))
    d_a = dt * -jnp.exp(a_log)
    tri = tri_ref[...]
    acum = jnp.dot(tri, d_a, precision=HIGHEST, preferred_element_type=F32)
    a_end = jnp.dot(seg_ref[...], d_a, precision=HIGHEST, preferred_element_type=F32)
    acum_t = acum.T
    causal = tri > 0.5
    e_acum = jnp.exp(acum)
    e_end = jnp.exp(a_end)
    d_end = jnp.exp(a_end - acum)
    lane = lax.broadcasted_iota(jnp.int32, (CHUNK, LANES), 1)
    for grp in range(SSD_GROUPS):
        cb = _dot_nt(xbc_ref[C_TILE0 + grp].astype(BF16), xbc_ref[B_TILE0 + grp].astype(BF16))
        for half in range(2):
            pair = 2 * grp + half
            plo = pair * LANES
            xs = xbc_ref[pair]
            xd = xs * _pair_cols(dt, pair, lane)
            xd16 = xd.astype(BF16)
            yd = []
            for k in range(2):
                h = 2 * pair + k
                decay = jnp.exp(jnp.where(causal, acum[:, h:h + 1] - acum_t[h:h + 1, :], -jnp.inf))
                yd.append(_dot((cb * decay).astype(BF16), xd16))
            y_ref[:, plo:plo + LANES] = jnp.where(lane < SSD_HEADDIM, yd[0], yd[1]) + par_ref[2:3, plo:plo + LANES] * xs
            xddt_ref[plo:plo + LANES, :] = (xd * _pair_cols(d_end, pair, lane)).T
    return e_acum, e_end


def _ssd_finish(y_ref, z_ref, par_ref, o_ref):
    o_ref[...] = _rms(y_ref[...] * _silu(z_ref[...]), par_ref[3:4, :]).astype(o_ref.dtype)


def _ssd_prompt_body(z_ref, x_ref, b_ref, c_ref, dt_ref, cw_ref, par_ref, tri_ref, seg_ref,
                     o_ref, st_ref, ext_ref, xbc_ref, y_ref, xddt_ref):
    first = pl.program_id(1) == 0

    @pl.when(first)
    def _():
        ext_ref[:, 0:8, :] = jnp.zeros((XBC_TILES, 8, LANES), F32)
        st_ref[...] = jnp.zeros_like(st_ref)

    for t in range(XBC_TILES):
        lo = t * LANES
        if t < B_TILE0:
            ext_ref[t, 8:8 + CHUNK, :] = x_ref[:, lo:lo + LANES]
        elif t < C_TILE0:
            ext_ref[t, 8:8 + CHUNK, :] = b_ref[:, lo - D_INNER:lo - D_INNER + LANES]
        else:
            ext_ref[t, 8:8 + CHUNK, :] = c_ref[:, lo - D_INNER - BC_DIM:lo - D_INNER - BC_DIM + LANES]

    for t in range(XBC_TILES):
        acc = cw_ref[t, CONV_W:CONV_W + 1, :] + ext_ref[t, 5:5 + CHUNK, :] * cw_ref[t, 0:1, :]
        for k in range(1, CONV_W):
            acc = acc + ext_ref[t, 5 + k:5 + k + CHUNK, :] * cw_ref[t, k:k + 1, :]
        xbc_ref[t] = _silu(acc)
    ext_ref[:, 0:8, :] = ext_ref[:, CHUNK:CHUNK + 8, :]

    e_acum, e_end = _ssd_intra(xbc_ref, dt_ref[...], par_ref, tri_ref, seg_ref, y_ref, xddt_ref)

    lane = lax.broadcasted_iota(jnp.int32, (CHUNK, LANES), 1)
    gw = SSD_STATE * 2
    for grp in range(SSD_GROUPS):
        rows = slice(grp * gw, (grp + 1) * gw)
        prev = st_ref[rows, :]
        y_off = _dot_nt(xbc_ref[C_TILE0 + grp].astype(BF16), prev.astype(BF16))
        for half in range(2):
            pair = 2 * grp + half
            plo = pair * LANES
            y_ref[:, plo:plo + LANES] += y_off[:, half * LANES:(half + 1) * LANES] * _pair_cols(e_acum, pair, lane)
        new = _dot(xddt_ref[rows, :].astype(BF16), xbc_ref[B_TILE0 + grp].astype(BF16))
        for k in range(4):
            h = 4 * grp + k
            hr = slice(grp * gw + k * SSD_HEADDIM, grp * gw + (k + 1) * SSD_HEADDIM)
            dec = jnp.broadcast_to(jnp.broadcast_to(e_end[:, h:h + 1], (CHUNK, SSD_STATE))[0:1, :],
                                   (SSD_HEADDIM, SSD_STATE))
            st_ref[hr, :] = prev[k * SSD_HEADDIM:(k + 1) * SSD_HEADDIM, :] * dec + new[k * SSD_HEADDIM:(k + 1) * SSD_HEADDIM, :]

    _ssd_finish(y_ref, z_ref, par_ref, o_ref)


def _ssd_consts(segment):
    r = jnp.arange(CHUNK)
    same = (r[:, None] // segment) == (r[None, :] // segment)
    return (same & (r[None, :] <= r[:, None])).astype(F32), same.astype(F32)


def _ssd_prompt(u, dt_raw, conv_wb, par, b, s):
    assert s % CHUNK == 0
    tri, seg = _ssd_consts(CHUNK)
    u3 = u.reshape(b, s, U_A)
    col = lambda width, blk: pl.BlockSpec((None, CHUNK, width), lambda i, c: (i, c, blk))
    y, st = pl.pallas_call(
        _ssd_prompt_body,
        grid=(b, s // CHUNK),
        in_specs=[col(D_INNER, 0), col(D_INNER, 1), col(BC_DIM, 2 * D_INNER // BC_DIM),
                  col(BC_DIM, 2 * D_INNER // BC_DIM + 1), col(LANES, 0),
                  _const_spec((XBC_TILES, 8, LANES)), _const_spec((4, D_INNER)),
                  _const_spec((CHUNK, CHUNK)), _const_spec((CHUNK, CHUNK))],
        out_specs=[pl.BlockSpec((None, CHUNK, D_INNER), lambda i, c: (i, c, 0)),
                   pl.BlockSpec((None, D_INNER, SSD_STATE), lambda i, c: (i, 0, 0))],
        out_shape=[jax.ShapeDtypeStruct((b, s, D_INNER), BF16),
                   jax.ShapeDtypeStruct((b, D_INNER, SSD_STATE), F32)],
        scratch_shapes=[pltpu.VMEM((XBC_TILES, CHUNK + 8, LANES), F32), pltpu.VMEM((XBC_TILES, CHUNK, LANES), F32),
                        pltpu.VMEM((CHUNK, D_INNER), F32), pltpu.VMEM((D_INNER, CHUNK), F32)],
        compiler_params=_params("parallel", "arbitrary"),
        name="ssd_prompt",
    )(u3, u3, u3, u3, dt_raw.reshape(b, s, LANES), conv_wb, par, tri, seg)
    return y.reshape(b * s, D_INNER), st


def _conv_sample_body(seq_ref, cw_ref, o_ref, *, tlen):
    acc = cw_ref[4:5, :] + seq_ref[:, 0:tlen, :] * cw_ref[0:1, :]
    for k in range(1, CONV_W):
        acc = acc + seq_ref[:, k:k + tlen, :] * cw_ref[k:k + 1, :]
    o_ref[...] = _silu(acc)


def _conv_sample(seq, conv_wb, tlen):
    b = seq.shape[0]
    bt = 8
    return pl.pallas_call(
        functools.partial(_conv_sample_body, tlen=tlen),
        grid=(b // bt,),
        in_specs=[pl.BlockSpec((bt, CONV_W - 1 + tlen, CONV_DIM), lambda i: (i, 0, 0)),
                  _const_spec((8, CONV_DIM))],
        out_specs=pl.BlockSpec((bt, tlen, CONV_DIM), lambda i: (i, 0, 0)),
        out_shape=jax.ShapeDtypeStruct((b, tlen, CONV_DIM), F32),
        compiler_params=_params("parallel"),
        name="conv_sample",
    )(seq, conv_wb)


SSD_BT = 8


def _ssd_sample_body(z_ref, xbc_in_ref, dt_ref, par_ref, tri_ref, seg_ref, st_in_ref,
                     o_ref, st_out_ref, xbc_ref, dtp_ref, zp_ref, y_ref, xddt_ref, op_ref, *, tlen):
    rows = SSD_BT * tlen
    xbc_ref[...] = jnp.zeros_like(xbc_ref)
    dtp_ref[...] = jnp.zeros_like(dtp_ref)
    zp_ref[...] = jnp.zeros_like(zp_ref)
    for t in range(XBC_TILES):
        xbc_ref[t, 0:rows, :] = xbc_in_ref[:, t * LANES:(t + 1) * LANES]
    dtp_ref[0:rows, :] = dt_ref[...]
    zp_ref[0:rows, :] = z_ref[...]
    e_acum, e_end = _ssd_intra(xbc_ref, dtp_ref[...], par_ref, tri_ref, seg_ref, y_ref, xddt_ref)

    lane = lax.broadcasted_iota(jnp.int32, (CHUNK, LANES), 1)
    row_b = lax.broadcasted_iota(jnp.int32, (CHUNK, 1), 0) // tlen
    win = 16
    per_win = win // tlen
    win_b = lax.broadcasted_iota(jnp.int32, (win, 1), 0) // tlen
    gw = SSD_STATE * 2
    e_end_b = [jnp.broadcast_to(e_end[:, h:h + 1], (CHUNK, SSD_STATE)) for h in range(SSD_HEADS)]
    for grp in range(SSD_GROUPS):
        rsl = slice(grp * gw, (grp + 1) * gw)
        c16 = xbc_ref[C_TILE0 + grp].astype(BF16)
        b32 = xbc_ref[B_TILE0 + grp]
        xddt16 = xddt_ref[rsl, :].astype(BF16)
        for w in range(rows // win):
            c_win = c16[w * win:(w + 1) * win, :]
            y_off = jnp.zeros((win, gw), F32)
            for k in range(per_win):
                bb = w * per_win + k
                prev = st_in_ref[bb, rsl, :]
                y_off = jnp.where(win_b == k, _dot_nt(c_win, prev.astype(BF16)), y_off)
                new = _dot(xddt16, jnp.where(row_b == bb, b32, 0.0).astype(BF16))
                for hh in range(4):
                    h = 4 * grp + hh
                    hs = slice(hh * SSD_HEADDIM, (hh + 1) * SSD_HEADDIM)
                    dec = jnp.broadcast_to(e_end_b[h][bb * tlen:bb * tlen + 1, :], (SSD_HEADDIM, SSD_STATE))
                    st_out_ref[bb, grp * gw + hh * SSD_HEADDIM:grp * gw + (hh + 1) * SSD_HEADDIM, :] = (
                        prev[hs, :] * dec + new[hs, :])
            for half in range(2):
                pair = 2 * grp + half
                plo = pair * LANES
                scale = _pair_cols(e_acum, pair, lane)[w * win:(w + 1) * win, :]
                y_ref[w * win:(w + 1) * win, plo:plo + LANES] += y_off[:, half * LANES:(half + 1) * LANES] * scale

    _ssd_finish(y_ref, zp_ref, par_ref, op_ref)
    o_ref[...] = op_ref[0:rows, :].astype(o_ref.dtype)


def _ssd_sample(u, xbc, dt_raw, par, state, layer, b, tlen):
    rows = SSD_BT * tlen
    assert CHUNK % rows == 0 and 16 % tlen == 0 and rows % 16 == 0
    tri, seg = _ssd_consts(tlen)
    row = lambda width, blk: pl.BlockSpec((rows, width), lambda i: (i, blk))
    st_spec = pl.BlockSpec((SSD_BT, D_INNER, SSD_STATE), lambda i: (i, 0, 0))
    st_in_spec = pl.BlockSpec((None, SSD_BT, D_INNER, SSD_STATE), lambda i: (layer, i, 0, 0))
    y, st = pl.pallas_call(
        functools.partial(_ssd_sample_body, tlen=tlen),
        grid=(b // SSD_BT,),
        in_specs=[row(D_INNER, 0), row(CONV_DIM, 0), row(LANES, 0), _const_spec((4, D_INNER)),
                  _const_spec((CHUNK, CHUNK)), _const_spec((CHUNK, CHUNK)), st_in_spec],
        out_specs=[row(D_INNER, 0), st_spec],
        out_shape=[jax.ShapeDtypeStruct((b * tlen, D_INNER), BF16),
                   jax.ShapeDtypeStruct((b, D_INNER, SSD_STATE), F32)],
        scratch_shapes=[pltpu.VMEM((XBC_TILES, CHUNK, LANES), F32), pltpu.VMEM((CHUNK, LANES), F32),
                        pltpu.VMEM((CHUNK, D_INNER), F32), pltpu.VMEM((CHUNK, D_INNER), F32),
                        pltpu.VMEM((D_INNER, CHUNK), F32), pltpu.VMEM((CHUNK, D_INNER), F32)],
        compiler_params=_params("parallel"),
        name="ssd_sample",
    )(u, xbc, dt_raw, par, tri, seg, state)
    return y, st


def _prep_weights(norm_g, w_ffn_gu, w_ffn_down, w_in_a, conv_w, conv_b, dt_bias, a_log, d_skip, ssd_norm_g,
                  w_out_a, w_in_b, w_out_b, kv_norm_g, w_kv_shared, mem_norm_g, w_mem_kv, final_norm_g):
    p = {}
    p["g_ffn"] = [[jnp.stack([norm_g[l, 0 + 4 * k], norm_g[l, 1 + 4 * k], final_norm_g]) for k in range(2)]
                  for l in range(DEPTH)]
    p["w_gu"] = w_ffn_gu.astype(BF16)
    p["w_down"] = w_ffn_down.astype(BF16)
    p["w_in_a"] = jnp.concatenate([w_in_a[:, :, :SSD_MAIN], w_in_a[:, :, SSD_IN:]], axis=-1).astype(BF16)
    p["w_dt"] = jnp.pad(w_in_a[:, :, SSD_MAIN:SSD_IN], ((0, 0), (0, 0), (0, LANES - SSD_HEADS))).astype(BF16)
    p["conv_wb"] = jnp.concatenate([conv_w, conv_b[:, None, :], jnp.zeros((N_A, 3, CONV_DIM), F32)], axis=1)
    p["conv_wb_tiles"] = p["conv_wb"].reshape(N_A, 8, XBC_TILES, LANES).transpose(0, 2, 1, 3)
    lane_pad = lambda v: jnp.pad(v, ((0, 0), (0, D_INNER - v.shape[1])))
    p["ssd_par"] = jnp.stack([lane_pad(dt_bias), lane_pad(a_log), jnp.repeat(d_skip, SSD_HEADDIM, axis=1),
                              ssd_norm_g], axis=1)
    p["w_out_a"] = w_out_a.astype(BF16)
    q_dil = N_DIL * DIL_GW
    p["w_in_b"] = jnp.concatenate([w_in_b[:, :, q_dil:], w_in_b[:, :, :q_dil]], axis=-1).astype(BF16)
    p["w_out_b"] = w_out_b.astype(BF16)
    p["w_kv"] = w_kv_shared.astype(BF16)
    p["w_mem_kv"] = w_mem_kv.astype(BF16)
    n = N_DIL * DIL_QH
    p["slopes"] = jnp.exp2(-8.0 * jnp.arange(1, n + 1, dtype=F32) / n)
    p["norm_g"] = norm_g
    p["kv_norm_g"] = kv_norm_g
    p["mem_norm_g"] = mem_norm_g
    return p


def _trunk(p, x, b, s, mem_kv, conv_prev, ssm_prev, win_bufs):
    prompt = win_bufs is None
    ssm_new, conv_new = [], []
    kv = None
    for l in range(DEPTH):
        g = p["norm_g"][l]
        x = _ffn(x, p["g_ffn"][l][0], p["w_gu"][l, 0], p["w_down"][l, 0])
        if l < N_A:
            u = _norm_matmul(x, g[2], p["w_in_a"][l])
            dt_raw = _norm_matmul(x, g[2], p["w_dt"][l])
            u3 = u.reshape(b, s, U_A)
            if prompt:
                y_mix, st = _ssd_prompt(u, dt_raw, p["conv_wb_tiles"][l], p["ssd_par"][l], b, s)
                conv_new.append(u3[:, s - (CONV_W - 1):, D_INNER:SSD_MAIN])
                mem_o = _mem_attn_prompt(u, SSD_MAIN // MEM_Q, mem_kv, l, b, s)
            else:
                seq = jnp.concatenate([conv_prev[l], u3[:, :, D_INNER:SSD_MAIN]], axis=1)
                xbc = _conv_sample(seq, p["conv_wb"][l], s).reshape(b * s, CONV_DIM)
                y_mix, st = _ssd_sample(u, xbc, dt_raw, p["ssd_par"][l], ssm_prev, l, b, s)
                conv_new.append(seq[:, s:])
                mem_o = _mem_attn_sample(u, SSD_MAIN // MEM_Q, mem_kv, l, b, s)
            ssm_new.append(st.reshape(b, SSD_HEADS, SSD_HEADDIM, SSD_STATE))
            x = _out_proj_a(y_mix, mem_o, x, g[3], p["w_out_a"][l])
        else:
            i = l - N_A
            u = _norm_matmul(x, g[2], p["w_in_b"][i])
            outs, lses = [], []
            for gi in range(N_DIL):
                if prompt:
                    o, lse = _dil_attn_prompt(p["slopes"], u, kv, gi, b, s)
                else:
                    o, lse = _dil_attn_sample(p["slopes"], u, kv, win_bufs[gi], gi, b, s)
                outs.append(o)
                lses.append(lse)
            if prompt:
                mem_o = _mem_attn_prompt(u, 0, mem_kv, l, b, s)
            else:
                mem_o = _mem_attn_sample(u, 0, mem_kv, l, b, s)
            x = _out_proj_b(outs, lses, mem_o, x, g[3], p["w_out_b"][i])
        x = _ffn(x, p["g_ffn"][l][1], p["w_gu"][l, 1], p["w_down"][l, 1], final=(l == DEPTH - 1))
        if l == N_A - 1:
            kv = _norm_matmul(x, p["kv_norm_g"], p["w_kv"])
    return x, jnp.stack(conv_new), jnp.stack(ssm_new), kv


def kernel(x_prompt, x_sample, mem_prompt, cache_mem_kv, state_ssm, state_conv, cache_win_g1, cache_win_g2,
           cache_win_g3, norm_g, w_ffn_gu, w_ffn_down, w_in_a, conv_w, conv_b, dt_bias, a_log, d_skip,
           ssd_norm_g, w_out_a, w_in_b, w_out_b, kv_norm_g, w_kv_shared, mem_norm_g, w_mem_kv, final_norm_g):
    p = _prep_weights(norm_g, w_ffn_gu, w_ffn_down, w_in_a, conv_w, conv_b, dt_bias, a_log, d_skip, ssd_norm_g,
                      w_out_a, w_in_b, w_out_b, kv_norm_g, w_kv_shared, mem_norm_g, w_mem_kv, final_norm_g)
    bp, sp = x_prompt.shape[:2]
    bs, ss = x_sample.shape[:2]

    mem_flat = mem_prompt.reshape(bp * N_MEM, D_MODEL)
    mem_kv_p = jnp.stack([_norm_matmul(mem_flat, mem_norm_g[l], p["w_mem_kv"][l]).reshape(bp, N_MEM, 2 * MEM_Q)
                          for l in range(DEPTH)])
    y_p, conv_p, ssm_p, kv_p = _trunk(p, x_prompt.reshape(bp * sp, D_MODEL), bp, sp, mem_kv_p, None, None, None)
    kv_p = kv_p.reshape(bp, sp, KV_W)
    win_p = [kv_p[:, sp - min(w, sp):, gi * DIL_GW:(gi + 1) * DIL_GW].reshape(bp, min(w, sp), 2, DIL_KVH, DIL_HD)
             for gi, (w, _) in enumerate(DIL_CONFIGS)]

    y_s, conv_s, ssm_s, kv_s = _trunk(p, x_sample.reshape(bs * ss, D_MODEL), bs, ss, cache_mem_kv, state_conv,
                                      state_ssm.reshape(N_A, bs, D_INNER, SSD_STATE),
                                      (cache_win_g1, cache_win_g2, cache_win_g3))
    kv_s = kv_s.reshape(bs, ss, N_DIL, 2, DIL_KVH, DIL_HD)
    return (y_p.reshape(bp, sp, D_MODEL), y_s.reshape(bs, ss, D_MODEL),
            mem_kv_p.reshape(DEPTH, bp, N_MEM, 2, MEM_HEADS, MEM_HD), ssm_p, conv_p, win_p[0], win_p[1], win_p[2],
            ssm_s, conv_s, kv_s[:, :, 0], kv_s[:, :, 1], kv_s[:, :, 2])
```

```python
import functools

import jax
import jax.numpy as jnp
from jax import lax
from jax.experimental import pallas as pl
from jax.experimental.pallas import tpu as pltpu

F32 = jnp.float32
BF16 = jnp.bfloat16
HIGHEST = lax.Precision.HIGHEST

D_MODEL = 1024
DEPTH = 4
N_A = 2
D_FF = 2816
D_INNER = 2048
SSD_HEADDIM = 64
SSD_HEADS = 32
SSD_GROUPS = 8
SSD_STATE = 128
CONV_W = 4
CONV_DIM = D_INNER + 2 * SSD_GROUPS * SSD_STATE
BC_DIM = SSD_GROUPS * SSD_STATE
DIL_CONFIGS = ((128, 1), (512, 4), (2048, 16))
N_DIL = 3
DIL_QH = 4
DIL_KVH = 2
DIL_HD = 128
DIL_NK = 128
DIL_GW = DIL_QH * DIL_HD
N_MEM = 256
MEM_HEADS = 4
MEM_HD = 256
MEM_Q = MEM_HEADS * MEM_HD
EPS = 1e-6
SSD_MAIN = 2 * D_INNER + 2 * BC_DIM
SSD_IN = SSD_MAIN + SSD_HEADS
U_ZQ = D_INNER + MEM_Q
U_XD = CONV_DIM + 128
U_B = MEM_Q + N_DIL * DIL_GW
KV_W = N_DIL * 2 * DIL_KVH * DIL_HD

LANES = 128
CHUNK = 128
FF_CHUNK = 256
VMEM_LIMIT = 56 * 2 ** 20


def _params(*sem):
    return pltpu.CompilerParams(dimension_semantics=sem, vmem_limit_bytes=VMEM_LIMIT)


def _const_spec(shape):
    return pl.BlockSpec(shape, lambda *_: (0,) * len(shape), pipeline_mode=pl.Buffered(1))


def _rms(x, g):
    return x * lax.rsqrt(jnp.mean(x * x, axis=-1, keepdims=True) + EPS) * g


def _silu(x):
    return x / (1.0 + jnp.exp(-x))


def _dot(a, b):
    return jnp.dot(a, b, preferred_element_type=F32)


def _dot_nt(a, b):
    return lax.dot_general(a, b, (((1,), (1,)), ((), ())), preferred_element_type=F32)


def _row_tile(t, pref):
    return pref if t % pref == 0 else t


def _ffn_body(x_ref, g_ref, wgu_ref, wd_ref, o_ref, acc_ref, *, final):
    x = x_ref[...]
    h = _rms(x, g_ref[0:1, :]).astype(BF16)
    for c in range(D_FF // FF_CHUNK):
        lo = c * FF_CHUNK
        gate = _dot(h, wgu_ref[:, lo:lo + FF_CHUNK])
        up = _dot(h, wgu_ref[:, D_FF + lo:D_FF + lo + FF_CHUNK])
        part = _dot((_silu(gate) * up).astype(BF16), wd_ref[lo:lo + FF_CHUNK, :])
        if c == 0:
            acc_ref[...] = part
        else:
            acc_ref[...] += part
    y = x + 0.5 * _rms(acc_ref[...], g_ref[1:2, :])
    if final:
        y = _rms(y, g_ref[2:3, :])
    o_ref[...] = y


def _pick_spec(shape, *lead):
    return pl.BlockSpec((None,) * len(lead) + tuple(shape), lambda *_: tuple(lead) + (0,) * len(shape),
                        pipeline_mode=pl.Buffered(1))


def _ffn(x, g3, wgu, wd, layer, which, final=False):
    t = x.shape[0]
    tm = _row_tile(t, 512)
    return pl.pallas_call(
        functools.partial(_ffn_body, final=final),
        grid=(t // tm,),
        in_specs=[pl.BlockSpec((tm, D_MODEL), lambda i: (i, 0)),
                  _const_spec((3, D_MODEL)),
                  _pick_spec((D_MODEL, 2 * D_FF), layer, which),
                  _pick_spec((D_FF, D_MODEL), layer, which)],
        out_specs=pl.BlockSpec((tm, D_MODEL), lambda i: (i, 0)),
        out_shape=jax.ShapeDtypeStruct((t, D_MODEL), F32),
        scratch_shapes=[pltpu.VMEM((tm, D_MODEL), F32)],
        compiler_params=_params("parallel"),
        name="ffn",
    )(x, g3, wgu, wd)


def _nmm_body(x_ref, g_ref, w_ref, o_ref, h_ref):
    @pl.when(pl.program_id(2) == 0)
    def _():
        h_ref[...] = _rms(x_ref[...], g_ref[...]).astype(BF16)

    o_ref[...] = _dot(h_ref[...], w_ref[...]).astype(o_ref.dtype)


def _col_tile(n):
    if n <= 2560:
        return n
    return max(c for c in range(LANES, 1536 + 1, LANES) if n % c == 0)


def _norm_matmul(x, g, w, layers=None, out_dtype=F32):
    t, n = x.shape[0], w.shape[2]
    layers = tuple(range(w.shape[0])) if layers is None else tuple(layers)
    first, count = layers[0], len(layers)
    assert layers == tuple(range(first, first + count))
    tm = _row_tile(t, 1024)
    tn = _col_tile(n)
    return pl.pallas_call(
        _nmm_body,
        grid=(count, t // tm, n // tn),
        in_specs=[pl.BlockSpec((tm, D_MODEL), lambda l, i, j: (i, 0)),
                  pl.BlockSpec((None, 1, D_MODEL), lambda l, i, j: (first + l, 0, 0)),
                  pl.BlockSpec((None, D_MODEL, tn), lambda l, i, j: (first + l, 0, j))],
        out_specs=pl.BlockSpec((None, tm, tn), lambda l, i, j: (l, i, j)),
        out_shape=jax.ShapeDtypeStruct((count, t, n), out_dtype),
        scratch_shapes=[pltpu.VMEM((tm, D_MODEL), BF16)],
        compiler_params=_params("parallel", "parallel", "arbitrary"),
        name="norm_matmul",
    )(x, g.reshape(g.shape[0], 1, D_MODEL), w)


def _out_a_body(y_ref, m_ref, x_ref, g_ref, w_ref, o_ref):
    acc = _dot(y_ref[...], w_ref[0:D_INNER, :])
    acc += _dot(m_ref[...], w_ref[D_INNER:D_INNER + MEM_Q, :])
    o_ref[...] = x_ref[...] + _rms(acc, g_ref[...])


def _out_proj_a(y, m, x, g, w, layer):
    t = x.shape[0]
    tm = _row_tile(t, 512)
    row = lambda width: pl.BlockSpec((tm, width), lambda i: (i, 0))
    return pl.pallas_call(
        _out_a_body,
        grid=(t // tm,),
        in_specs=[row(D_INNER), row(MEM_Q), row(D_MODEL), _const_spec((1, D_MODEL)),
                  _pick_spec((D_INNER + MEM_Q, D_MODEL), layer)],
        out_specs=row(D_MODEL),
        out_shape=jax.ShapeDtypeStruct((t, D_MODEL), F32),
        compiler_params=_params("parallel"),
        name="out_proj_a",
    )(y, m, x, g.reshape(1, D_MODEL), w)


def _out_b_body(o1_ref, o2_ref, o3_ref, l1_ref, l2_ref, l3_ref, m_ref, x_ref, g_ref, w_ref, o_ref):
    l1, l2, l3 = l1_ref[...], l2_ref[...], l3_ref[...]
    mx = jnp.maximum(jnp.maximum(l1, l2), l3)
    e1, e2, e3 = jnp.exp(l1 - mx), jnp.exp(l2 - mx), jnp.exp(l3 - mx)
    den = e1 + e2 + e3
    a1, a2, a3 = e1 / den, e2 / den, e3 / den
    mix = []
    for h in range(DIL_QH):
        c = slice(h * DIL_HD, (h + 1) * DIL_HD)
        mix.append(a1[:, h:h + 1] * o1_ref[:, c] + a2[:, h:h + 1] * o2_ref[:, c] + a3[:, h:h + 1] * o3_ref[:, c])
    acc = _dot(jnp.concatenate(mix, axis=1).astype(BF16), w_ref[0:DIL_GW, :])
    acc += _dot(m_ref[...], w_ref[DIL_GW:DIL_GW + MEM_Q, :])
    o_ref[...] = x_ref[...] + _rms(acc, g_ref[...])


def _out_proj_b(outs, lses, m, x, g, w, layer):
    t = x.shape[0]
    tm = _row_tile(t, 512)
    row = lambda width: pl.BlockSpec((tm, width), lambda i: (i, 0))
    return pl.pallas_call(
        _out_b_body,
        grid=(t // tm,),
        in_specs=[row(DIL_GW)] * 3 + [row(LANES)] * 3 + [row(MEM_Q), row(D_MODEL), _const_spec((1, D_MODEL)),
                                                         _pick_spec((DIL_GW + MEM_Q, D_MODEL), layer)],
        out_specs=row(D_MODEL),
        out_shape=jax.ShapeDtypeStruct((t, D_MODEL), F32),
        compiler_params=_params("parallel"),
        name="out_proj_b",
    )(*outs, *lses, m, x, g.reshape(1, D_MODEL), w)


def _softmax_rows(s):
    e = jnp.exp(s - jnp.max(s, axis=-1, keepdims=True))
    return e / jnp.sum(e, axis=-1, keepdims=True)


def _mem_prompt_body(q_ref, kv_ref, o_ref):
    cols = [slice(h * MEM_HD, (h + 1) * MEM_HD) for h in range(MEM_HEADS)]
    scores = [_dot_nt(q_ref[:, c].astype(BF16), kv_ref[:, c].astype(BF16)) * (MEM_HD ** -0.5) for c in cols]
    probs = [_softmax_rows(s).astype(BF16) for s in scores]
    for h, c in enumerate(cols):
        v = kv_ref[:, MEM_Q + h * MEM_HD:MEM_Q + (h + 1) * MEM_HD].astype(BF16)
        o_ref[:, c] = _dot(probs[h], v).astype(o_ref.dtype)


def _mem_attn_prompt(u, q_block, mem_kv, layer, b, s):
    c = u.shape[1]
    tq = _row_tile(s, 1024)
    out = pl.pallas_call(
        _mem_prompt_body,
        grid=(b, s // tq),
        in_specs=[pl.BlockSpec((None, tq, MEM_Q), lambda i, j: (i, j, q_block)),
                  pl.BlockSpec((None, None, N_MEM, 2 * MEM_Q), lambda i, j: (layer, i, 0, 0))],
        out_specs=pl.BlockSpec((None, tq, MEM_Q), lambda i, j: (i, j, 0)),
        out_shape=jax.ShapeDtypeStruct((b, s, MEM_Q), BF16),
        compiler_params=_params("parallel", "parallel"),
        name="mem_attn_prompt",
    )(u.reshape(b, s, c), mem_kv)
    return out.reshape(b * s, MEM_Q)


MEM_BT = 4
MEM_LT = MEM_HD // LANES
MEM_ROWS = 2 * MEM_LT * MEM_HEADS


def _mem_rows(kv_ref, bb, kv, h):
    parts = [kv_ref[bb, pl.ds((kv * MEM_LT + lt) * MEM_HEADS + h, N_MEM, stride=MEM_ROWS), :]
             for lt in range(MEM_LT)]
    return jnp.concatenate(parts, axis=1)


def _mem_sample_body(q_ref, kv_ref, o_ref, *, tlen):
    rows = MEM_BT * tlen
    owner = lax.broadcasted_iota(jnp.int32, (rows, 1), 0) // tlen
    q_all = q_ref[...]
    for h in range(MEM_HEADS):
        lo = h * MEM_HD
        q = q_all[:, lo:lo + MEM_HD].astype(BF16)
        s = jnp.zeros((rows, N_MEM), F32)
        for bb in range(MEM_BT):
            k = _mem_rows(kv_ref, bb, 0, h).astype(BF16)
            s = jnp.where(owner == bb, _dot_nt(q, k), s)
        p = _softmax_rows(s * (MEM_HD ** -0.5)).astype(BF16)
        o = jnp.zeros((rows, MEM_HD), F32)
        for bb in range(MEM_BT):
            v = _mem_rows(kv_ref, bb, 1, h).astype(BF16)
            o = jnp.where(owner == bb, _dot(p, v), o)
        o_ref[:, lo:lo + MEM_HD] = o.astype(o_ref.dtype)


def _mem_attn_sample(u, q_block, mem_kv, layer, b, tlen):
    rows = MEM_BT * tlen
    depth = mem_kv.shape[0]
    mem_kv = mem_kv.reshape(depth, b, N_MEM, 2, MEM_HEADS, MEM_LT, LANES).transpose(0, 1, 2, 3, 5, 4, 6)
    mem_kv = mem_kv.reshape(depth, b, N_MEM * MEM_ROWS, LANES)
    return pl.pallas_call(
        functools.partial(_mem_sample_body, tlen=tlen),
        grid=(b // MEM_BT,),
        in_specs=[pl.BlockSpec((rows, MEM_Q), lambda i: (i, q_block)),
                  pl.BlockSpec((None, MEM_BT, N_MEM * MEM_ROWS, LANES), lambda i: (layer, i, 0, 0))],
        out_specs=pl.BlockSpec((rows, MEM_Q), lambda i: (i, 0)),
        out_shape=jax.ShapeDtypeStruct((b * tlen, MEM_Q), BF16),
        compiler_params=_params("parallel"),
        name="mem_attn_sample",
    )(u, mem_kv)


DIL_TB = 2048


def _dil_prompt_body(sl_ref, q_ref, kp_ref, vp_ref, kc_ref, vc_ref, o_ref, l_ref, *, gi, dil):
    span = DIL_NK * dil
    has_prev = pl.program_id(1) > 0
    head = pl.program_id(2)
    qi = lax.broadcasted_iota(jnp.int32, (DIL_NK, DIL_NK), 0)
    kj = lax.broadcasted_iota(jnp.int32, (DIL_NK, DIL_NK), 1)
    du_prev = DIL_NK + qi - kj
    du_own = qi - kj
    valid_prev = du_prev <= DIL_NK
    valid_own = du_own >= 0
    my_lane = kj == head
    neg_slope = -sl_ref[gi * DIL_QH + head]

    @pl.when(head == 0)
    def _():
        l_ref[...] = jnp.zeros_like(l_ref)

    bias_prev = neg_slope * (dil * du_prev).astype(F32)
    bias_own = neg_slope * (dil * du_own).astype(F32)
    scale = DIL_HD ** -0.5

    def rows(start):
        return pl.ds(start, DIL_NK) if dil == 1 else pl.ds(start, DIL_NK, stride=dil)

    def scores(blk, r):
        own = rows(blk * span + r)
        if blk == 0:
            k_prev, v_prev, ok_prev = kp_ref[rows(r), :], vp_ref[rows(r), :], valid_prev & has_prev
        else:
            prev = rows((blk - 1) * span + r)
            k_prev, v_prev, ok_prev = kc_ref[prev, :], vc_ref[prev, :], valid_prev
        q = q_ref[own, :].astype(BF16)
        s_prev = jnp.where(ok_prev, _dot_nt(q, k_prev.astype(BF16)) * scale + bias_prev, -jnp.inf)
        s_own = jnp.where(valid_own, _dot_nt(q, kc_ref[own, :].astype(BF16)) * scale + bias_own, -jnp.inf)
        mx = jnp.maximum(jnp.max(s_prev, axis=-1, keepdims=True), jnp.max(s_own, axis=-1, keepdims=True))
        return own, s_prev, s_own, mx, v_prev

    def weights(own, s_prev, s_own, mx, v_prev):
        e_prev = jnp.exp(s_prev - mx)
        e_own = jnp.exp(s_own - mx)
        den = jnp.sum(e_prev, axis=-1, keepdims=True) + jnp.sum(e_own, axis=-1, keepdims=True)
        return own, (e_prev / den).astype(BF16), (e_own / den).astype(BF16), mx + jnp.log(den), v_prev

    def finish(own, p_prev, p_own, lse, v_prev):
        o_ref[own, :] = _dot(p_prev, v_prev.astype(BF16)) + _dot(p_own, vc_ref[own, :].astype(BF16))
        l_ref[own, :] = jnp.where(my_lane, lse, l_ref[own, :])

    def streams(group):
        for st in [weights(*sc) for sc in [scores(blk, r) for blk, r in group]]:
            finish(*st)

    together = 8
    if dil <= together:
        todo = [(blk, r) for blk in range(DIL_TB // span) for r in range(dil)]
        for g0 in range(0, len(todo), together):
            streams(todo[g0:g0 + together])
    else:
        for blk in range(DIL_TB // span):
            def per_group(g, carry, blk=blk):
                streams([(blk, g * together + k) for k in range(together)])
                return carry

            lax.fori_loop(0, dil // together, per_group, 0)


def _dil_attn_prompt(slopes, u, kv, gi, b, s):
    window, dil = DIL_CONFIGS[gi]
    span = DIL_NK * dil
    assert window // dil == DIL_NK and s % DIL_TB == 0 and DIL_TB % span == 0
    per_tb = DIL_TB // span
    qpk = DIL_QH // DIL_KVH
    q0 = (MEM_Q + gi * DIL_GW) // DIL_HD
    k0 = gi * DIL_GW // DIL_HD
    v0 = k0 + DIL_KVH
    blk = lambda imap: pl.BlockSpec((None, DIL_TB, DIL_HD), imap)
    tail = lambda imap: pl.BlockSpec((None, span, DIL_HD), imap)
    u3 = u.reshape(b, s, U_B)
    kv3 = kv.reshape(b, s, KV_W)
    o, lse = pl.pallas_call(
        functools.partial(_dil_prompt_body, gi=gi, dil=dil),
        grid=(b, s // DIL_TB, DIL_QH),
        in_specs=[pl.BlockSpec(memory_space=pltpu.SMEM),
                  blk(lambda i, j, h: (i, j, q0 + h)),
                  tail(lambda i, j, h: (i, jnp.maximum(j * per_tb - 1, 0), k0 + h // qpk)),
                  tail(lambda i, j, h: (i, jnp.maximum(j * per_tb - 1, 0), v0 + h // qpk)),
                  blk(lambda i, j, h: (i, j, k0 + h // qpk)),
                  blk(lambda i, j, h: (i, j, v0 + h // qpk))],
        out_specs=[blk(lambda i, j, h: (i, j, h)),
                   pl.BlockSpec((None, DIL_TB, LANES), lambda i, j, h: (i, j, 0))],
        out_shape=[jax.ShapeDtypeStruct((b, s, DIL_GW), F32), jax.ShapeDtypeStruct((b, s, LANES), F32)],
        compiler_params=_params("parallel", "parallel", "arbitrary"),
        name=f"dil_attn_prompt_g{gi}",
    )(slopes, u3, kv3, kv3, kv3, kv3)
    return o.reshape(b * s, DIL_GW), lse.reshape(b * s, LANES)


DIL_BT = 4


def _dil_sample_body(sl_ref, q_ref, buf_ref, new_ref, o_ref, l_ref, *, gi, dil, tlen):
    mi = lax.broadcasted_iota(jnp.int32, (DIL_NK, 1), 0)
    si = lax.broadcasted_iota(jnp.int32, (tlen, 1), 0)
    head_lane = lax.broadcasted_iota(jnp.int32, (1, LANES), 1)
    scale = DIL_HD ** -0.5
    if len(buf_ref.shape) == 4:
        per_pos = buf_ref.shape[2]
        buf_ref = buf_ref.reshape(DIL_BT, DIL_NK * per_pos, DIL_HD)
    else:
        per_pos = buf_ref.shape[1] // DIL_NK

    def per_batch(bb, carry):
        q_all = q_ref[bb]
        new = new_ref[bb]
        for t in range(tlen):
            if dil == 1:
                res = 0
                j_buf, ok_buf = DIL_NK + t - mi, mi >= t
                j_new, ok_new = t - si, si <= t
            else:
                res = t
                j_buf, ok_buf = DIL_NK - mi, mi >= 0
                j_new, ok_new = jnp.zeros_like(si), si == t
            dist_buf = (dil * j_buf).astype(F32)
            dist_new = (dil * j_new).astype(F32)
            lse_row = jnp.zeros((1, LANES), F32)
            for kvh in range(DIL_KVH):
                klo = kvh * DIL_HD
                vlo = DIL_KVH * DIL_HD + klo
                k_buf = buf_ref[bb, pl.ds((res * 2 + 0) * DIL_KVH + kvh, DIL_NK, stride=per_pos), :]
                v_buf = buf_ref[bb, pl.ds((res * 2 + 1) * DIL_KVH + kvh, DIL_NK, stride=per_pos), :]
                k_new = new[:, klo:klo + DIL_HD]
                v_new = new[:, vlo:vlo + DIL_HD]
                for qh in range(DIL_QH // DIL_KVH):
                    head = kvh * (DIL_QH // DIL_KVH) + qh
                    lo = head * DIL_HD
                    neg_slope = -sl_ref[gi * DIL_QH + head]
                    q = q_all[t:t + 1, lo:lo + DIL_HD]
                    s_buf = jnp.sum(q * k_buf, axis=-1, keepdims=True) * scale + neg_slope * dist_buf
                    s_new = jnp.sum(q * k_new, axis=-1, keepdims=True) * scale + neg_slope * dist_new
                    s_buf = jnp.where(ok_buf, s_buf, -jnp.inf)
                    s_new = jnp.where(ok_new, s_new, -jnp.inf)
                    mx = jnp.maximum(jnp.max(s_buf, axis=0, keepdims=True), jnp.max(s_new, axis=0, keepdims=True))
                    e_buf = jnp.exp(s_buf - mx)
                    e_new = jnp.exp(s_new - mx)
                    den = jnp.sum(e_buf, axis=0, keepdims=True) + jnp.sum(e_new, axis=0, keepdims=True)
                    acc = jnp.sum(e_buf * v_buf, axis=0, keepdims=True) + jnp.sum(e_new * v_new, axis=0, keepdims=True)
                    o_ref[bb, t:t + 1, lo:lo + DIL_HD] = acc / den
                    lse_row = jnp.where(head_lane == head, mx + jnp.log(den), lse_row)
            l_ref[bb, t:t + 1, :] = lse_row
        return carry

    lax.fori_loop(0, DIL_BT, per_batch, 0)


def _dil_attn_sample(slopes, u, kv_new, win_buf, gi, b, tlen):
    window, dil = DIL_CONFIGS[gi]
    lb = win_buf.shape[1]
    assert window // dil == DIL_NK and lb == window and (dil == 1 or tlen <= dil)
    per_res = 2 * DIL_KVH
    fetch = 1 if dil == 1 else tlen
    if fetch == dil:
        buf = win_buf.reshape(b, DIL_NK * dil * per_res, DIL_HD)
        buf_spec = pl.BlockSpec((DIL_BT, DIL_NK * dil * per_res, DIL_HD), lambda i: (i, 0, 0))
    else:
        assert (fetch * per_res) % 8 == 0
        buf = win_buf.reshape(b, DIL_NK, dil * per_res, DIL_HD)
        buf_spec = pl.BlockSpec((DIL_BT, DIL_NK, fetch * per_res, DIL_HD), lambda i: (i, 0, 0, 0))
    o, lse = pl.pallas_call(
        functools.partial(_dil_sample_body, gi=gi, dil=dil, tlen=tlen),
        grid=(b // DIL_BT,),
        in_specs=[pl.BlockSpec(memory_space=pltpu.SMEM),
                  pl.BlockSpec((DIL_BT, tlen, DIL_GW), lambda i: (i, 0, MEM_Q // DIL_GW + gi)),
                  buf_spec,
                  pl.BlockSpec((DIL_BT, tlen, DIL_GW), lambda i: (i, 0, gi))],
        out_specs=[pl.BlockSpec((DIL_BT, tlen, DIL_GW), lambda i: (i, 0, 0)),
                   pl.BlockSpec((DIL_BT, tlen, LANES), lambda i: (i, 0, 0))],
        out_shape=[jax.ShapeDtypeStruct((b, tlen, DIL_GW), F32), jax.ShapeDtypeStruct((b, tlen, LANES), F32)],
        compiler_params=_params("parallel"),
        name=f"dil_attn_sample_g{gi}",
    )(slopes, u.reshape(b, tlen, U_B), buf, kv_new.reshape(b, tlen, KV_W))
    return o.reshape(b * tlen, DIL_GW), lse.reshape(b * tlen, LANES)


def _pair_cols(a, pair, lane):
    h0 = 2 * pair
    return jnp.where(lane < SSD_HEADDIM, a[:, h0:h0 + 1], a[:, h0 + 1:h0 + 2])


X_TILES = D_INNER // LANES
B_TILE0 = X_TILES
C_TILE0 = X_TILES + BC_DIM // LANES
XBC_TILES = CONV_DIM // LANES


def _ssd_intra(xbc_ref, dt_raw, par_ref, tri_ref, seg_ref, y_ref, xddt_ref):
    dt_bias, a_log = par_ref[0:1, 0:LANES], par_ref[1:2, 0:LANES]
    pre = dt_raw + dt_bias
    dt = jnp.maximum(pre, 0.0) + jnp.log(1.0 + jnp.exp(-jnp.abs(pre)))
    d_a = dt * -jnp.exp(a_log)
    tri = tri_ref[...]
    acum = jnp.dot(tri, d_a, precision=HIGHEST, preferred_element_type=F32)
    a_end = jnp.dot(seg_ref[...], d_a, precision=HIGHEST, preferred_element_type=F32)
    acum_t = acum.T
    causal = tri > 0.5
    e_acum = jnp.exp(acum)
    e_end = jnp.exp(a_end)
    d_end = jnp.exp(a_end - acum)
    lane = lax.broadcasted_iota(jnp.int32, (CHUNK, LANES), 1)
    for grp in range(SSD_GROUPS):
        cb = _dot_nt(xbc_ref[C_TILE0 + grp].astype(BF16), xbc_ref[B_TILE0 + grp].astype(BF16))
        for half in range(2):
            pair = 2 * grp + half
            plo = pair * LANES
            xs = xbc_ref[pair]
            xd = xs * _pair_cols(dt, pair, lane)
            xd16 = xd.astype(BF16)
            yd = []
            for k in range(2):
                h = 2 * pair + k
                decay = jnp.exp(jnp.where(causal, acum[:, h:h + 1] - acum_t[h:h + 1, :], -jnp.inf))
                yd.append(_dot((cb * decay).astype(BF16), xd16))
            y_ref[:, plo:plo + LANES] = jnp.where(lane < SSD_HEADDIM, yd[0], yd[1]) + par_ref[2:3, plo:plo + LANES] * xs
            xddt_ref[plo:plo + LANES, :] = (xd * _pair_cols(d_end, pair, lane)).T
    return e_acum, e_end


def _ssd_finish(y_ref, z_ref, par_ref, o_ref):
    o_ref[...] = _rms(y_ref[...] * _silu(z_ref[...].astype(F32)), par_ref[3:4, :]).astype(o_ref.dtype)


def _ssd_prompt_body(z_ref, x_ref, b_ref, c_ref, dt_ref, cw_ref, par_ref, tri_ref, seg_ref,
                     o_ref, st_ref, ext_ref, xbc_ref, y_ref, xddt_ref):
    first = pl.program_id(1) == 0

    @pl.when(first)
    def _():
        ext_ref[:, 0:8, :] = jnp.zeros((XBC_TILES, 8, LANES), F32)
        st_ref[...] = jnp.zeros_like(st_ref)

    for t in range(XBC_TILES):
        lo = t * LANES
        if t < B_TILE0:
            ext_ref[t, 8:8 + CHUNK, :] = x_ref[:, lo:lo + LANES]
        elif t < C_TILE0:
            ext_ref[t, 8:8 + CHUNK, :] = b_ref[:, lo - D_INNER:lo - D_INNER + LANES]
        else:
            ext_ref[t, 8:8 + CHUNK, :] = c_ref[:, lo - D_INNER - BC_DIM:lo - D_INNER - BC_DIM + LANES]

    for t in range(XBC_TILES):
        acc = cw_ref[t, CONV_W:CONV_W + 1, :] + ext_ref[t, 5:5 + CHUNK, :] * cw_ref[t, 0:1, :]
        for k in range(1, CONV_W):
            acc = acc + ext_ref[t, 5 + k:5 + k + CHUNK, :] * cw_ref[t, k:k + 1, :]
        xbc_ref[t] = _silu(acc)
    ext_ref[:, 0:8, :] = ext_ref[:, CHUNK:CHUNK + 8, :]

    e_acum, e_end = _ssd_intra(xbc_ref, dt_ref[...], par_ref, tri_ref, seg_ref, y_ref, xddt_ref)

    lane = lax.broadcasted_iota(jnp.int32, (CHUNK, LANES), 1)
    gw = SSD_STATE * 2
    for grp in range(SSD_GROUPS):
        rows = slice(grp * gw, (grp + 1) * gw)
        prev = st_ref[rows, :]
        y_off = _dot_nt(xbc_ref[C_TILE0 + grp].astype(BF16), prev.astype(BF16))
        for half in range(2):
            pair = 2 * grp + half
            plo = pair * LANES
            y_ref[:, plo:plo + LANES] += y_off[:, half * LANES:(half + 1) * LANES] * _pair_cols(e_acum, pair, lane)
        new = _dot(xddt_ref[rows, :].astype(BF16), xbc_ref[B_TILE0 + grp].astype(BF16))
        for k in range(4):
            h = 4 * grp + k
            hr = slice(grp * gw + k * SSD_HEADDIM, grp * gw + (k + 1) * SSD_HEADDIM)
            dec = jnp.broadcast_to(jnp.broadcast_to(e_end[:, h:h + 1], (CHUNK, SSD_STATE))[0:1, :],
                                   (SSD_HEADDIM, SSD_STATE))
            st_ref[hr, :] = prev[k * SSD_HEADDIM:(k + 1) * SSD_HEADDIM, :] * dec + new[k * SSD_HEADDIM:(k + 1) * SSD_HEADDIM, :]

    _ssd_finish(y_ref, z_ref, par_ref, o_ref)


def _ssd_consts(segment):
    r = jnp.arange(CHUNK)
    same = (r[:, None] // segment) == (r[None, :] // segment)
    return (same & (r[None, :] <= r[:, None])).astype(F32), same.astype(F32)


def _ssd_prompt(u_zq, u_xd, conv_wb, par, b, s):
    assert s % CHUNK == 0
    tri, seg = _ssd_consts(CHUNK)
    zq3 = u_zq.reshape(b, s, U_ZQ)
    xd3 = u_xd.reshape(b, s, U_XD)
    col = lambda width, blk: pl.BlockSpec((None, CHUNK, width), lambda i, c: (i, c, blk))
    y, st = pl.pallas_call(
        _ssd_prompt_body,
        grid=(b, s // CHUNK),
        in_specs=[col(D_INNER, 0), col(D_INNER, 0), col(BC_DIM, D_INNER // BC_DIM),
                  col(BC_DIM, D_INNER // BC_DIM + 1), col(LANES, CONV_DIM // LANES),
                  _const_spec((XBC_TILES, 8, LANES)), _const_spec((4, D_INNER)),
                  _const_spec((CHUNK, CHUNK)), _const_spec((CHUNK, CHUNK))],
        out_specs=[pl.BlockSpec((None, CHUNK, D_INNER), lambda i, c: (i, c, 0)),
                   pl.BlockSpec((None, D_INNER, SSD_STATE), lambda i, c: (i, 0, 0))],
        out_shape=[jax.ShapeDtypeStruct((b, s, D_INNER), BF16),
                   jax.ShapeDtypeStruct((b, D_INNER, SSD_STATE), F32)],
        scratch_shapes=[pltpu.VMEM((XBC_TILES, CHUNK + 8, LANES), F32), pltpu.VMEM((XBC_TILES, CHUNK, LANES), F32),
                        pltpu.VMEM((CHUNK, D_INNER), F32), pltpu.VMEM((D_INNER, CHUNK), F32)],
        compiler_params=_params("parallel", "arbitrary"),
        name="ssd_prompt",
    )(zq3, xd3, xd3, xd3, xd3, conv_wb, par, tri, seg)
    return y.reshape(b * s, D_INNER), st


def _conv_sample_body(seq_ref, cw_ref, o_ref, *, tlen):
    acc = cw_ref[4:5, :] + seq_ref[:, 0:tlen, :] * cw_ref[0:1, :]
    for k in range(1, CONV_W):
        acc = acc + seq_ref[:, k:k + tlen, :] * cw_ref[k:k + 1, :]
    o_ref[...] = _silu(acc)


def _conv_sample(seq, conv_wb, tlen):
    b = seq.shape[0]
    bt = 8
    return pl.pallas_call(
        functools.partial(_conv_sample_body, tlen=tlen),
        grid=(b // bt,),
        in_specs=[pl.BlockSpec((bt, CONV_W - 1 + tlen, CONV_DIM), lambda i: (i, 0, 0)),
                  _const_spec((8, CONV_DIM))],
        out_specs=pl.BlockSpec((bt, tlen, CONV_DIM), lambda i: (i, 0, 0)),
        out_shape=jax.ShapeDtypeStruct((b, tlen, CONV_DIM), F32),
        compiler_params=_params("parallel"),
        name="conv_sample",
    )(seq, conv_wb)


SSD_BT = 8


def _ssd_sample_body(z_ref, xbc_in_ref, dt_ref, par_ref, tri_ref, seg_ref, st_in_ref,
                     o_ref, st_out_ref, xbc_ref, dtp_ref, zp_ref, y_ref, xddt_ref, op_ref, *, tlen):
    rows = SSD_BT * tlen
    xbc_ref[...] = jnp.zeros_like(xbc_ref)
    dtp_ref[...] = jnp.zeros_like(dtp_ref)
    zp_ref[...] = jnp.zeros_like(zp_ref)
    for t in range(XBC_TILES):
        xbc_ref[t, 0:rows, :] = xbc_in_ref[:, t * LANES:(t + 1) * LANES]
    dtp_ref[0:rows, :] = dt_ref[...]
    zp_ref[0:rows, :] = z_ref[...].astype(F32)
    e_acum, e_end = _ssd_intra(xbc_ref, dtp_ref[...], par_ref, tri_ref, seg_ref, y_ref, xddt_ref)

    lane = lax.broadcasted_iota(jnp.int32, (CHUNK, LANES), 1)
    row_b = lax.broadcasted_iota(jnp.int32, (CHUNK, 1), 0) // tlen
    win = 16
    per_win = win // tlen
    win_b = lax.broadcasted_iota(jnp.int32, (win, 1), 0) // tlen
    gw = SSD_STATE * 2
    e_end_b = [jnp.broadcast_to(e_end[:, h:h + 1], (CHUNK, SSD_STATE)) for h in range(SSD_HEADS)]
    for grp in range(SSD_GROUPS):
        rsl = slice(grp * gw, (grp + 1) * gw)
        c16 = xbc_ref[C_TILE0 + grp].astype(BF16)
        b32 = xbc_ref[B_TILE0 + grp]
        xddt16 = xddt_ref[rsl, :].astype(BF16)
        for w in range(rows // win):
            c_win = c16[w * win:(w + 1) * win, :]
            y_off = jnp.zeros((win, gw), F32)
            for k in range(per_win):
                bb = w * per_win + k
                prev = st_in_ref[bb, rsl, :]
                y_off = jnp.where(win_b == k, _dot_nt(c_win, prev.astype(BF16)), y_off)
                new = _dot(xddt16, jnp.where(row_b == bb, b32, 0.0).astype(BF16))
                for hh in range(4):
                    h = 4 * grp + hh
                    hs = slice(hh * SSD_HEADDIM, (hh + 1) * SSD_HEADDIM)
                    dec = jnp.broadcast_to(e_end_b[h][bb * tlen:bb * tlen + 1, :], (SSD_HEADDIM, SSD_STATE))
                    st_out_ref[bb, grp * gw + hh * SSD_HEADDIM:grp * gw + (hh + 1) * SSD_HEADDIM, :] = (
                        prev[hs, :] * dec + new[hs, :])
            for half in range(2):
                pair = 2 * grp + half
                plo = pair * LANES
                scale = _pair_cols(e_acum, pair, lane)[w * win:(w + 1) * win, :]
                y_ref[w * win:(w + 1) * win, plo:plo + LANES] += y_off[:, half * LANES:(half + 1) * LANES] * scale

    _ssd_finish(y_ref, zp_ref, par_ref, op_ref)
    o_ref[...] = op_ref[0:rows, :].astype(o_ref.dtype)


def _ssd_sample(u_zq, xbc, u_xd, par, state, layer, b, tlen):
    rows = SSD_BT * tlen
    assert CHUNK % rows == 0 and 16 % tlen == 0 and rows % 16 == 0
    tri, seg = _ssd_consts(tlen)
    row = lambda width, blk: pl.BlockSpec((rows, width), lambda i: (i, blk))
    st_spec = pl.BlockSpec((SSD_BT, D_INNER, SSD_STATE), lambda i: (i, 0, 0))
    st_in_spec = pl.BlockSpec((None, SSD_BT, D_INNER, SSD_STATE), lambda i: (layer, i, 0, 0))
    y, st = pl.pallas_call(
        functools.partial(_ssd_sample_body, tlen=tlen),
        grid=(b // SSD_BT,),
        in_specs=[row(D_INNER, 0), row(CONV_DIM, 0), row(LANES, CONV_DIM // LANES), _const_spec((4, D_INNER)),
                  _const_spec((CHUNK, CHUNK)), _const_spec((CHUNK, CHUNK)), st_in_spec],
        out_specs=[row(D_INNER, 0), st_spec],
        out_shape=[jax.ShapeDtypeStruct((b * tlen, D_INNER), BF16),
                   jax.ShapeDtypeStruct((b, D_INNER, SSD_STATE), F32)],
        scratch_shapes=[pltpu.VMEM((XBC_TILES, CHUNK, LANES), F32), pltpu.VMEM((CHUNK, LANES), F32),
                        pltpu.VMEM((CHUNK, D_INNER), F32), pltpu.VMEM((CHUNK, D_INNER), F32),
                        pltpu.VMEM((D_INNER, CHUNK), F32), pltpu.VMEM((CHUNK, D_INNER), F32)],
        compiler_params=_params("parallel"),
        name="ssd_sample",
    )(u_zq, xbc, u_xd, par, tri, seg, state)
    return y, st


def _prep_weights(norm_g, w_ffn_gu, w_ffn_down, w_in_a, conv_w, conv_b, dt_bias, a_log, d_skip, ssd_norm_g,
                  w_out_a, w_in_b, w_out_b, kv_norm_g, w_kv_shared, mem_norm_g, w_mem_kv, final_norm_g):
    p = {}
    p["g_ffn"] = [[jnp.stack([norm_g[l, 0 + 4 * k], norm_g[l, 1 + 4 * k], final_norm_g]) for k in range(2)]
                  for l in range(DEPTH)]
    p["w_gu"] = w_ffn_gu.astype(BF16)
    p["w_down"] = w_ffn_down.astype(BF16)
    p["w_zq"] = jnp.concatenate([w_in_a[:, :, :D_INNER], w_in_a[:, :, SSD_IN:]], axis=-1).astype(BF16)
    w_dt = jnp.pad(w_in_a[:, :, SSD_MAIN:SSD_IN], ((0, 0), (0, 0), (0, LANES - SSD_HEADS)))
    p["w_xd"] = jnp.concatenate([w_in_a[:, :, D_INNER:SSD_MAIN], w_dt], axis=-1).astype(BF16)
    p["g_in_a"] = norm_g[:N_A, 2]
    p["g_in_b"] = norm_g[N_A:, 2]
    p["conv_wb"] = jnp.concatenate([conv_w, conv_b[:, None, :], jnp.zeros((N_A, 3, CONV_DIM), F32)], axis=1)
    p["conv_wb_tiles"] = p["conv_wb"].reshape(N_A, 8, XBC_TILES, LANES).transpose(0, 2, 1, 3)
    lane_pad = lambda v: jnp.pad(v, ((0, 0), (0, D_INNER - v.shape[1])))
    p["ssd_par"] = jnp.stack([lane_pad(dt_bias), lane_pad(a_log), jnp.repeat(d_skip, SSD_HEADDIM, axis=1),
                              ssd_norm_g], axis=1)
    p["w_out_a"] = w_out_a.astype(BF16)
    q_dil = N_DIL * DIL_GW
    p["w_in_b"] = jnp.concatenate([w_in_b[:, :, q_dil:], w_in_b[:, :, :q_dil]], axis=-1).astype(BF16)
    p["w_out_b"] = w_out_b.astype(BF16)
    p["w_kv"] = w_kv_shared.astype(BF16)
    p["w_mem_kv"] = w_mem_kv.astype(BF16)
    n = N_DIL * DIL_QH
    p["slopes"] = jnp.exp2(-8.0 * jnp.arange(1, n + 1, dtype=F32) / n)
    p["norm_g"] = norm_g
    p["kv_norm_g"] = kv_norm_g
    p["mem_norm_g"] = mem_norm_g
    return p


def _trunk(p, x, b, s, mem_kv, conv_prev, ssm_prev, win_bufs):
    prompt = win_bufs is None
    ssm_new, conv_new = [], []
    kv = None
    for l in range(DEPTH):
        g = p["norm_g"][l]
        x = _ffn(x, p["g_ffn"][l][0], p["w_gu"], p["w_down"], l, 0)
        if l < N_A:
            u_zq = _norm_matmul(x, p["g_in_a"], p["w_zq"], layers=(l,), out_dtype=BF16)[0]
            u_xd = _norm_matmul(x, p["g_in_a"], p["w_xd"], layers=(l,))[0]
            raw = u_xd.reshape(b, s, U_XD)
            if prompt:
                y_mix, st = _ssd_prompt(u_zq, u_xd, p["conv_wb_tiles"][l], p["ssd_par"][l], b, s)
                conv_new.append(raw[:, s - (CONV_W - 1):, :CONV_DIM])
                mem_o = _mem_attn_prompt(u_zq, D_INNER // MEM_Q, mem_kv, l, b, s)
            else:
                seq = jnp.concatenate([conv_prev[l], raw[:, :, :CONV_DIM]], axis=1)
                xbc = _conv_sample(seq, p["conv_wb"][l], s).reshape(b * s, CONV_DIM)
                y_mix, st = _ssd_sample(u_zq, xbc, u_xd, p["ssd_par"][l], ssm_prev, l, b, s)
                conv_new.append(seq[:, s:])
                mem_o = _mem_attn_sample(u_zq, D_INNER // MEM_Q, mem_kv, l, b, s)
            ssm_new.append(st.reshape(b, SSD_HEADS, SSD_HEADDIM, SSD_STATE))
            x = _out_proj_a(y_mix, mem_o, x, g[3], p["w_out_a"], l)
        else:
            i = l - N_A
            u = _norm_matmul(x, p["g_in_b"], p["w_in_b"], layers=(i,))[0]
            outs, lses = [], []
            for gi in range(N_DIL):
                if prompt:
                    o, lse = _dil_attn_prompt(p["slopes"], u, kv, gi, b, s)
                else:
                    o, lse = _dil_attn_sample(p["slopes"], u, kv, win_bufs[gi], gi, b, s)
                outs.append(o)
                lses.append(lse)
            if prompt:
                mem_o = _mem_attn_prompt(u, 0, mem_kv, l, b, s)
            else:
                mem_o = _mem_attn_sample(u, 0, mem_kv, l, b, s)
            x = _out_proj_b(outs, lses, mem_o, x, g[3], p["w_out_b"], i)
        x = _ffn(x, p["g_ffn"][l][1], p["w_gu"], p["w_down"], l, 1, final=(l == DEPTH - 1))
        if l == N_A - 1:
            kv = _norm_matmul(x, p["kv_norm_g"][None], p["w_kv"][None])[0]
    return x, jnp.stack(conv_new), jnp.stack(ssm_new), kv


def kernel(x_prompt, x_sample, mem_prompt, cache_mem_kv, state_ssm, state_conv, cache_win_g1, cache_win_g2,
           cache_win_g3, norm_g, w_ffn_gu, w_ffn_down, w_in_a, conv_w, conv_b, dt_bias, a_log, d_skip,
           ssd_norm_g, w_out_a, w_in_b, w_out_b, kv_norm_g, w_kv_shared, mem_norm_g, w_mem_kv, final_norm_g):
    p = _prep_weights(norm_g, w_ffn_gu, w_ffn_down, w_in_a, conv_w, conv_b, dt_bias, a_log, d_skip, ssd_norm_g,
                      w_out_a, w_in_b, w_out_b, kv_norm_g, w_kv_shared, mem_norm_g, w_mem_kv, final_norm_g)
    bp, sp = x_prompt.shape[:2]
    bs, ss = x_sample.shape[:2]

    mem_flat = mem_prompt.reshape(bp * N_MEM, D_MODEL)
    mem_kv_p = _norm_matmul(mem_flat, mem_norm_g, p["w_mem_kv"]).reshape(DEPTH, bp, N_MEM, 2 * MEM_Q)
    y_p, conv_p, ssm_p, kv_p = _trunk(p, x_prompt.reshape(bp * sp, D_MODEL), bp, sp, mem_kv_p, None, None, None)
    kv_p = kv_p.reshape(bp, sp, KV_W)
    win_p = [kv_p[:, sp - min(w, sp):, gi * DIL_GW:(gi + 1) * DIL_GW].reshape(bp, min(w, sp), 2, DIL_KVH, DIL_HD)
             for gi, (w, _) in enumerate(DIL_CONFIGS)]

    y_s, conv_s, ssm_s, kv_s = _trunk(p, x_sample.reshape(bs * ss, D_MODEL), bs, ss, cache_mem_kv, state_conv,
                                      state_ssm.reshape(N_A, bs, D_INNER, SSD_STATE),
                                      (cache_win_g1, cache_win_g2, cache_win_g3))
    kv_s = kv_s.reshape(bs, ss, N_DIL, 2, DIL_KVH, DIL_HD)
    return (y_p.reshape(bp, sp, D_MODEL), y_s.reshape(bs, ss, D_MODEL),
            mem_kv_p.reshape(DEPTH, bp, N_MEM, 2, MEM_HEADS, MEM_HD), ssm_p, conv_p, win_p[0], win_p[1], win_p[2],
            ssm_s, conv_s, kv_s[:, :, 0], kv_s[:, :, 1], kv_s[:, :, 2])
```

```python
import functools

import jax
import jax.numpy as jnp
from jax import lax
from jax.experimental import pallas as pl
from jax.experimental.pallas import tpu as pltpu

F32 = jnp.float32
BF16 = jnp.bfloat16
HIGHEST = lax.Precision.HIGHEST

D_MODEL = 1024
DEPTH = 4
N_A = 2
D_FF = 2816
D_INNER = 2048
SSD_HEADDIM = 64
SSD_HEADS = 32
SSD_GROUPS = 8
SSD_STATE = 128
CONV_W = 4
CONV_DIM = D_INNER + 2 * SSD_GROUPS * SSD_STATE
BC_DIM = SSD_GROUPS * SSD_STATE
DIL_CONFIGS = ((128, 1), (512, 4), (2048, 16))
N_DIL = 3
DIL_QH = 4
DIL_KVH = 2
DIL_HD = 128
DIL_NK = 128
DIL_GW = DIL_QH * DIL_HD
N_MEM = 256
MEM_HEADS = 4
MEM_HD = 256
MEM_Q = MEM_HEADS * MEM_HD
EPS = 1e-6
SSD_MAIN = 2 * D_INNER + 2 * BC_DIM
SSD_IN = SSD_MAIN + SSD_HEADS
U_ZQ = D_INNER + MEM_Q
U_XD = CONV_DIM + 128
U_B = MEM_Q + N_DIL * DIL_GW
KV_W = N_DIL * 2 * DIL_KVH * DIL_HD

LANES = 128
CHUNK = 128
FF_CHUNK = 256
VMEM_LIMIT = 56 * 2 ** 20


def _params(*sem):
    return pltpu.CompilerParams(dimension_semantics=sem, vmem_limit_bytes=VMEM_LIMIT)


def _const_spec(shape):
    return pl.BlockSpec(shape, lambda *_: (0,) * len(shape), pipeline_mode=pl.Buffered(1))


def _rms(x, g):
    return x * lax.rsqrt(jnp.mean(x * x, axis=-1, keepdims=True) + EPS) * g


def _silu(x):
    return x / (1.0 + jnp.exp(-x))


def _dot(a, b):
    return jnp.dot(a, b, preferred_element_type=F32)


def _dot_nt(a, b):
    return lax.dot_general(a, b, (((1,), (1,)), ((), ())), preferred_element_type=F32)


def _row_tile(t, pref):
    return pref if t % pref == 0 else t


def _ffn_body(x_ref, g_ref, wgu_ref, wd_ref, o_ref, acc_ref, *, final):
    x = x_ref[...]
    h = _rms(x, g_ref[0:1, :]).astype(BF16)
    for c in range(D_FF // FF_CHUNK):
        lo = c * FF_CHUNK
        gate = _dot(h, wgu_ref[:, lo:lo + FF_CHUNK])
        up = _dot(h, wgu_ref[:, D_FF + lo:D_FF + lo + FF_CHUNK])
        part = _dot((_silu(gate) * up).astype(BF16), wd_ref[lo:lo + FF_CHUNK, :])
        if c == 0:
            acc_ref[...] = part
        else:
            acc_ref[...] += part
    y = x + 0.5 * _rms(acc_ref[...], g_ref[1:2, :])
    if final:
        y = _rms(y, g_ref[2:3, :])
    o_ref[...] = y


def _pick_spec(shape, *lead):
    return pl.BlockSpec((None,) * len(lead) + tuple(shape), lambda *_: tuple(lead) + (0,) * len(shape),
                        pipeline_mode=pl.Buffered(1))


def _ffn(x, g3, wgu, wd, layer, which, final=False):
    t = x.shape[0]
    tm = _row_tile(t, 512)
    return pl.pallas_call(
        functools.partial(_ffn_body, final=final),
        grid=(t // tm,),
        in_specs=[pl.BlockSpec((tm, D_MODEL), lambda i: (i, 0)),
                  _const_spec((3, D_MODEL)),
                  _pick_spec((D_MODEL, 2 * D_FF), layer, which),
                  _pick_spec((D_FF, D_MODEL), layer, which)],
        out_specs=pl.BlockSpec((tm, D_MODEL), lambda i: (i, 0)),
        out_shape=jax.ShapeDtypeStruct((t, D_MODEL), F32),
        scratch_shapes=[pltpu.VMEM((tm, D_MODEL), F32)],
        compiler_params=_params("parallel"),
        name="ffn",
    )(x, g3, wgu, wd)


def _nmm_body(x_ref, g_ref, w_ref, o_ref, h_ref):
    @pl.when(pl.program_id(2) == 0)
    def _():
        h_ref[...] = _rms(x_ref[...], g_ref[...]).astype(BF16)

    o_ref[...] = _dot(h_ref[...], w_ref[...]).astype(o_ref.dtype)


def _col_tile(n):
    if n <= 2560:
        return n
    return max(c for c in range(LANES, 1536 + 1, LANES) if n % c == 0)


def _norm_matmul(x, g, w, layers=None, out_dtype=F32):
    t, n = x.shape[0], w.shape[2]
    layers = tuple(range(w.shape[0])) if layers is None else tuple(layers)
    first, count = layers[0], len(layers)
    assert layers == tuple(range(first, first + count))
    tm = _row_tile(t, 1024)
    tn = _col_tile(n)
    return pl.pallas_call(
        _nmm_body,
        grid=(count, t // tm, n // tn),
        in_specs=[pl.BlockSpec((tm, D_MODEL), lambda l, i, j: (i, 0)),
                  pl.BlockSpec((None, 1, D_MODEL), lambda l, i, j: (first + l, 0, 0)),
                  pl.BlockSpec((None, D_MODEL, tn), lambda l, i, j: (first + l, 0, j))],
        out_specs=pl.BlockSpec((None, tm, tn), lambda l, i, j: (l, i, j)),
        out_shape=jax.ShapeDtypeStruct((count, t, n), out_dtype),
        scratch_shapes=[pltpu.VMEM((tm, D_MODEL), BF16)],
        compiler_params=_params("parallel", "parallel", "arbitrary"),
        name="norm_matmul",
    )(x, g.reshape(g.shape[0], 1, D_MODEL), w)


def _out_a_body(y_ref, m_ref, x_ref, g_ref, w_ref, o_ref):
    acc = _dot(y_ref[...], w_ref[0:D_INNER, :])
    acc += _dot(m_ref[...], w_ref[D_INNER:D_INNER + MEM_Q, :])
    o_ref[...] = x_ref[...] + _rms(acc, g_ref[...])


def _out_proj_a(y, m, x, g, w, layer):
    t = x.shape[0]
    tm = _row_tile(t, 512)
    row = lambda width: pl.BlockSpec((tm, width), lambda i: (i, 0))
    return pl.pallas_call(
        _out_a_body,
        grid=(t // tm,),
        in_specs=[row(D_INNER), row(MEM_Q), row(D_MODEL), _const_spec((1, D_MODEL)),
                  _pick_spec((D_INNER + MEM_Q, D_MODEL), layer)],
        out_specs=row(D_MODEL),
        out_shape=jax.ShapeDtypeStruct((t, D_MODEL), F32),
        compiler_params=_params("parallel"),
        name="out_proj_a",
    )(y, m, x, g.reshape(1, D_MODEL), w)


def _out_b_body(o1_ref, o2_ref, o3_ref, l1_ref, l2_ref, l3_ref, m_ref, x_ref, g_ref, w_ref, o_ref):
    l1, l2, l3 = l1_ref[...], l2_ref[...], l3_ref[...]
    mx = jnp.maximum(jnp.maximum(l1, l2), l3)
    e1, e2, e3 = jnp.exp(l1 - mx), jnp.exp(l2 - mx), jnp.exp(l3 - mx)
    den = e1 + e2 + e3
    a1, a2, a3 = e1 / den, e2 / den, e3 / den
    mix = []
    for h in range(DIL_QH):
        c = slice(h * DIL_HD, (h + 1) * DIL_HD)
        mix.append(a1[:, h:h + 1] * o1_ref[:, c] + a2[:, h:h + 1] * o2_ref[:, c] + a3[:, h:h + 1] * o3_ref[:, c])
    acc = _dot(jnp.concatenate(mix, axis=1).astype(BF16), w_ref[0:DIL_GW, :])
    acc += _dot(m_ref[...], w_ref[DIL_GW:DIL_GW + MEM_Q, :])
    o_ref[...] = x_ref[...] + _rms(acc, g_ref[...])


def _out_proj_b(outs, lses, m, x, g, w, layer):
    t = x.shape[0]
    tm = _row_tile(t, 512)
    row = lambda width: pl.BlockSpec((tm, width), lambda i: (i, 0))
    return pl.pallas_call(
        _out_b_body,
        grid=(t // tm,),
        in_specs=[row(DIL_GW)] * 3 + [row(LANES)] * 3 + [row(MEM_Q), row(D_MODEL), _const_spec((1, D_MODEL)),
                                                         _pick_spec((DIL_GW + MEM_Q, D_MODEL), layer)],
        out_specs=row(D_MODEL),
        out_shape=jax.ShapeDtypeStruct((t, D_MODEL), F32),
        compiler_params=_params("parallel"),
        name="out_proj_b",
    )(*outs, *lses, m, x, g.reshape(1, D_MODEL), w)


def _softmax_rows(s):
    e = jnp.exp(s - jnp.max(s, axis=-1, keepdims=True))
    return e / jnp.sum(e, axis=-1, keepdims=True)


def _mem_prompt_body(q_ref, kv_ref, o_ref):
    cols = [slice(h * MEM_HD, (h + 1) * MEM_HD) for h in range(MEM_HEADS)]
    scores = [_dot_nt(q_ref[:, c].astype(BF16), kv_ref[:, c].astype(BF16)) * (MEM_HD ** -0.5) for c in cols]
    probs = [_softmax_rows(s).astype(BF16) for s in scores]
    for h, c in enumerate(cols):
        v = kv_ref[:, MEM_Q + h * MEM_HD:MEM_Q + (h + 1) * MEM_HD].astype(BF16)
        o_ref[:, c] = _dot(probs[h], v).astype(o_ref.dtype)


def _mem_attn_prompt(u, q_block, mem_kv, layer, b, s):
    c = u.shape[1]
    tq = _row_tile(s, 1024)
    out = pl.pallas_call(
        _mem_prompt_body,
        grid=(b, s // tq),
        in_specs=[pl.BlockSpec((None, tq, MEM_Q), lambda i, j: (i, j, q_block)),
                  pl.BlockSpec((None, None, N_MEM, 2 * MEM_Q), lambda i, j: (layer, i, 0, 0))],
        out_specs=pl.BlockSpec((None, tq, MEM_Q), lambda i, j: (i, j, 0)),
        out_shape=jax.ShapeDtypeStruct((b, s, MEM_Q), BF16),
        compiler_params=_params("parallel", "parallel"),
        name="mem_attn_prompt",
    )(u.reshape(b, s, c), mem_kv)
    return out.reshape(b * s, MEM_Q)


MEM_BT = 4
MEM_LT = MEM_HD // LANES
MEM_ROWS = 2 * MEM_LT * MEM_HEADS


def _mem_rows(kv_ref, bb, kv, h):
    parts = [kv_ref[bb, pl.ds((kv * MEM_LT + lt) * MEM_HEADS + h, N_MEM, stride=MEM_ROWS), :]
             for lt in range(MEM_LT)]
    return jnp.concatenate(parts, axis=1)


def _mem_sample_body(q_ref, kv_ref, o_ref, *, tlen):
    rows = MEM_BT * tlen
    owner = lax.broadcasted_iota(jnp.int32, (rows, 1), 0) // tlen
    q_all = q_ref[...]
    for h in range(MEM_HEADS):
        lo = h * MEM_HD
        q = q_all[:, lo:lo + MEM_HD].astype(BF16)
        s = jnp.zeros((rows, N_MEM), F32)
        for bb in range(MEM_BT):
            k = _mem_rows(kv_ref, bb, 0, h).astype(BF16)
            s = jnp.where(owner == bb, _dot_nt(q, k), s)
        p = _softmax_rows(s * (MEM_HD ** -0.5)).astype(BF16)
        o = jnp.zeros((rows, MEM_HD), F32)
        for bb in range(MEM_BT):
            v = _mem_rows(kv_ref, bb, 1, h).astype(BF16)
            o = jnp.where(owner == bb, _dot(p, v), o)
        o_ref[:, lo:lo + MEM_HD] = o.astype(o_ref.dtype)


def _mem_attn_sample(u, q_block, mem_kv, layer, b, tlen):
    rows = MEM_BT * tlen
    depth = mem_kv.shape[0]
    mem_kv = mem_kv.reshape(depth, b, N_MEM, 2, MEM_HEADS, MEM_LT, LANES).transpose(0, 1, 2, 3, 5, 4, 6)
    mem_kv = mem_kv.reshape(depth, b, N_MEM * MEM_ROWS, LANES)
    return pl.pallas_call(
        functools.partial(_mem_sample_body, tlen=tlen),
        grid=(b // MEM_BT,),
        in_specs=[pl.BlockSpec((rows, MEM_Q), lambda i: (i, q_block)),
                  pl.BlockSpec((None, MEM_BT, N_MEM * MEM_ROWS, LANES), lambda i: (layer, i, 0, 0))],
        out_specs=pl.BlockSpec((rows, MEM_Q), lambda i: (i, 0)),
        out_shape=jax.ShapeDtypeStruct((b * tlen, MEM_Q), BF16),
        compiler_params=_params("parallel"),
        name="mem_attn_sample",
    )(u, mem_kv)


DIL_TB = 2048


def _dil_prompt_body(sl_ref, q_ref, kp_ref, vp_ref, kc_ref, vc_ref, o_ref, l_ref, *, gi, dil):
    span = DIL_NK * dil
    has_prev = pl.program_id(1) > 0
    head = pl.program_id(2)
    qi = lax.broadcasted_iota(jnp.int32, (DIL_NK, DIL_NK), 0)
    kj = lax.broadcasted_iota(jnp.int32, (DIL_NK, DIL_NK), 1)
    du_prev = DIL_NK + qi - kj
    du_own = qi - kj
    valid_prev = du_prev <= DIL_NK
    valid_own = du_own >= 0
    my_lane = kj == head
    neg_slope = -sl_ref[gi * DIL_QH + head]

    @pl.when(head == 0)
    def _():
        l_ref[...] = jnp.zeros_like(l_ref)

    bias_prev = neg_slope * (dil * du_prev).astype(F32)
    bias_own = neg_slope * (dil * du_own).astype(F32)
    scale = DIL_HD ** -0.5

    def rows(start):
        return pl.ds(start, DIL_NK) if dil == 1 else pl.ds(start, DIL_NK, stride=dil)

    def scores(blk, r):
        own = rows(blk * span + r)
        if blk == 0:
            k_prev, v_prev, ok_prev = kp_ref[rows(r), :], vp_ref[rows(r), :], valid_prev & has_prev
        else:
            prev = rows((blk - 1) * span + r)
            k_prev, v_prev, ok_prev = kc_ref[prev, :], vc_ref[prev, :], valid_prev
        q = q_ref[own, :].astype(BF16)
        s_prev = jnp.where(ok_prev, _dot_nt(q, k_prev.astype(BF16)) * scale + bias_prev, -jnp.inf)
        s_own = jnp.where(valid_own, _dot_nt(q, kc_ref[own, :].astype(BF16)) * scale + bias_own, -jnp.inf)
        mx = jnp.maximum(jnp.max(s_prev, axis=-1, keepdims=True), jnp.max(s_own, axis=-1, keepdims=True))
        return own, s_prev, s_own, mx, v_prev

    def weights(own, s_prev, s_own, mx, v_prev):
        e_prev = jnp.exp(s_prev - mx)
        e_own = jnp.exp(s_own - mx)
        den = jnp.sum(e_prev, axis=-1, keepdims=True) + jnp.sum(e_own, axis=-1, keepdims=True)
        return own, (e_prev / den).astype(BF16), (e_own / den).astype(BF16), mx + jnp.log(den), v_prev

    def finish(own, p_prev, p_own, lse, v_prev):
        o_ref[own, :] = _dot(p_prev, v_prev.astype(BF16)) + _dot(p_own, vc_ref[own, :].astype(BF16))
        l_ref[own, :] = jnp.where(my_lane, lse, l_ref[own, :])

    def streams(group):
        for st in [weights(*sc) for sc in [scores(blk, r) for blk, r in group]]:
            finish(*st)

    together = 8
    if dil <= together:
        todo = [(blk, r) for blk in range(DIL_TB // span) for r in range(dil)]
        for g0 in range(0, len(todo), together):
            streams(todo[g0:g0 + together])
    else:
        for blk in range(DIL_TB // span):
            def per_group(g, carry, blk=blk):
                streams([(blk, g * together + k) for k in range(together)])
                return carry

            lax.fori_loop(0, dil // together, per_group, 0)


def _dil_attn_prompt(slopes, u, kv, gi, b, s):
    window, dil = DIL_CONFIGS[gi]
    span = DIL_NK * dil
    assert window // dil == DIL_NK and s % DIL_TB == 0 and DIL_TB % span == 0
    per_tb = DIL_TB // span
    qpk = DIL_QH // DIL_KVH
    q0 = (MEM_Q + gi * DIL_GW) // DIL_HD
    k0 = gi * DIL_GW // DIL_HD
    v0 = k0 + DIL_KVH
    blk = lambda imap: pl.BlockSpec((None, DIL_TB, DIL_HD), imap)
    tail = lambda imap: pl.BlockSpec((None, span, DIL_HD), imap)
    u3 = u.reshape(b, s, U_B)
    kv3 = kv.reshape(b, s, KV_W)
    o, lse = pl.pallas_call(
        functools.partial(_dil_prompt_body, gi=gi, dil=dil),
        grid=(b, s // DIL_TB, DIL_QH),
        in_specs=[pl.BlockSpec(memory_space=pltpu.SMEM),
                  blk(lambda i, j, h: (i, j, q0 + h)),
                  tail(lambda i, j, h: (i, jnp.maximum(j * per_tb - 1, 0), k0 + h // qpk)),
                  tail(lambda i, j, h: (i, jnp.maximum(j * per_tb - 1, 0), v0 + h // qpk)),
                  blk(lambda i, j, h: (i, j, k0 + h // qpk)),
                  blk(lambda i, j, h: (i, j, v0 + h // qpk))],
        out_specs=[blk(lambda i, j, h: (i, j, h)),
                   pl.BlockSpec((None, DIL_TB, LANES), lambda i, j, h: (i, j, 0))],
        out_shape=[jax.ShapeDtypeStruct((b, s, DIL_GW), F32), jax.ShapeDtypeStruct((b, s, LANES), F32)],
        compiler_params=_params("parallel", "parallel", "arbitrary"),
        name=f"dil_attn_prompt_g{gi}",
    )(slopes, u3, kv3, kv3, kv3, kv3)
    return o.reshape(b * s, DIL_GW), lse.reshape(b * s, LANES)


DIL_BT = 8
DIL_TOGETHER = 4


def _dil_sample_body(sl_ref, q_ref, buf_ref, new_ref, o_ref, l_ref, knew_ref, vnew_ref, *, gi, dil, tlen):
    rows = tlen * DIL_QH
    qpk = DIL_QH // DIL_KVH
    if len(buf_ref.shape) == 4:
        per_pos = buf_ref.shape[2]
        buf_ref = buf_ref.reshape(DIL_BT, DIL_NK * per_pos, DIL_HD)
    else:
        per_pos = buf_ref.shape[1] // DIL_NK
    ri = lax.broadcasted_iota(jnp.int32, (rows, 1), 0)
    tok, head = ri // DIL_QH, ri % DIL_QH
    kvh_of_row = head // qpk
    lane = lax.broadcasted_iota(jnp.int32, (rows, DIL_NK), 1)
    neg_slope = jnp.zeros((rows, 1), F32)
    for h in range(DIL_QH):
        neg_slope = jnp.where(head == h, -sl_ref[gi * DIL_QH + h], neg_slope)
    if dil == 1:
        j_buf, ok_buf = DIL_NK + tok - lane, lane >= tok
        j_new, ok_new = tok - lane, lane <= tok
    else:
        j_buf, ok_buf = DIL_NK - lane, None
        j_new, ok_new = jnp.zeros_like(lane), lane == tok
    bias_buf = neg_slope * (dil * j_buf).astype(F32)
    bias_new = neg_slope * (dil * j_new).astype(F32)
    scale = DIL_HD ** -0.5
    if dil == 1:
        tiles = [(0, kvh, kvh_of_row == kvh) for kvh in range(DIL_KVH)]
    else:
        tiles = [(t, kvh, (tok == t) & (kvh_of_row == kvh)) for t in range(tlen) for kvh in range(DIL_KVH)]
    first_kvh = kvh_of_row == 0
    knew_ref[...] = jnp.zeros_like(knew_ref)
    vnew_ref[...] = jnp.zeros_like(vnew_ref)

    def buf_tile(bb, res, kv, kvh):
        return buf_ref[bb, pl.ds((res * 2 + kv) * DIL_KVH + kvh, DIL_NK, stride=per_pos), :].astype(BF16)

    def scores(bb, slot):
        q = q_ref[bb].astype(BF16)
        new = new_ref[bb]
        for kvh in range(DIL_KVH):
            knew_ref[slot, kvh, 0:tlen, :] = new[:, kvh * DIL_HD:(kvh + 1) * DIL_HD]
            vnew_ref[slot, kvh, 0:tlen, :] = new[:, (DIL_KVH + kvh) * DIL_HD:(DIL_KVH + kvh + 1) * DIL_HD]
        s_buf = jnp.zeros((rows, DIL_NK), F32)
        for res, kvh, mine in tiles:
            s_buf = jnp.where(mine, _dot_nt(q, buf_tile(bb, res, 0, kvh)), s_buf)
        s_new = jnp.where(first_kvh, _dot_nt(q, knew_ref[slot, 0].astype(BF16)),
                          _dot_nt(q, knew_ref[slot, 1].astype(BF16)))
        return s_buf, s_new

    def weights(s_buf, s_new):
        s_buf = s_buf * scale + bias_buf
        if ok_buf is not None:
            s_buf = jnp.where(ok_buf, s_buf, -jnp.inf)
        s_new = jnp.where(ok_new, s_new * scale + bias_new, -jnp.inf)
        mx = jnp.maximum(jnp.max(s_buf, axis=-1, keepdims=True), jnp.max(s_new, axis=-1, keepdims=True))
        e_buf = jnp.exp(s_buf - mx)
        e_new = jnp.exp(s_new - mx)
        den = jnp.sum(e_buf, axis=-1, keepdims=True) + jnp.sum(e_new, axis=-1, keepdims=True)
        return (e_buf / den).astype(BF16), (e_new / den).astype(BF16), mx + jnp.log(den)

    def finish(bb, slot, p_buf, p_new, lse):
        o = jnp.where(first_kvh, _dot(p_new, vnew_ref[slot, 0].astype(BF16)),
                      _dot(p_new, vnew_ref[slot, 1].astype(BF16)))
        for res, kvh, mine in tiles:
            o = o + jnp.where(mine, _dot(p_buf, buf_tile(bb, res, 1, kvh)), 0.0)
        o_ref[bb] = o
        l_ref[bb] = jnp.broadcast_to(lse, (rows, DIL_HD))

    def per_group(g, carry):
        members = [(g * DIL_TOGETHER + k, k) for k in range(DIL_TOGETHER)]
        scored = [scores(bb, slot) for bb, slot in members]
        weighted = [weights(*sc) for sc in scored]
        for (bb, slot), w in zip(members, weighted):
            finish(bb, slot, *w)
        return carry

    lax.fori_loop(0, DIL_BT // DIL_TOGETHER, per_group, 0)


def _dil_attn_sample(slopes, u, kv_new, win_buf, gi, b, tlen):
    window, dil = DIL_CONFIGS[gi]
    lb = win_buf.shape[1]
    assert window // dil == DIL_NK and lb == window and (dil == 1 or tlen <= dil) and DIL_KVH == 2
    rows = tlen * DIL_QH
    per_res = 2 * DIL_KVH
    fetch = 1 if dil == 1 else tlen
    if fetch == dil:
        buf = win_buf.reshape(b, DIL_NK * dil * per_res, DIL_HD)
        buf_spec = pl.BlockSpec((DIL_BT, DIL_NK * dil * per_res, DIL_HD), lambda i: (i, 0, 0))
    else:
        assert (fetch * per_res) % 8 == 0
        buf = win_buf.reshape(b, DIL_NK, dil * per_res, DIL_HD)
        buf_spec = pl.BlockSpec((DIL_BT, DIL_NK, fetch * per_res, DIL_HD), lambda i: (i, 0, 0, 0))
    q0 = MEM_Q + gi * DIL_GW
    q = u[:, q0:q0 + DIL_GW].reshape(b, rows, DIL_HD)
    row_blk = pl.BlockSpec((DIL_BT, rows, DIL_HD), lambda i: (i, 0, 0))
    o, lse = pl.pallas_call(
        functools.partial(_dil_sample_body, gi=gi, dil=dil, tlen=tlen),
        grid=(b // DIL_BT,),
        in_specs=[pl.BlockSpec(memory_space=pltpu.SMEM), row_blk, buf_spec,
                  pl.BlockSpec((DIL_BT, tlen, DIL_GW), lambda i: (i, 0, gi))],
        out_specs=[row_blk, row_blk],
        out_shape=[jax.ShapeDtypeStruct((b, rows, DIL_HD), F32)] * 2,
        scratch_shapes=[pltpu.VMEM((DIL_TOGETHER, DIL_KVH, DIL_NK, DIL_HD), F32)] * 2,
        compiler_params=_params("parallel"),
        name=f"dil_attn_sample_g{gi}",
    )(slopes, q, buf, kv_new.reshape(b, tlen, KV_W))
    lse = jnp.pad(lse[:, :, 0].reshape(b * tlen, DIL_QH), ((0, 0), (0, LANES - DIL_QH)))
    return o.reshape(b * tlen, DIL_GW), lse


def _pair_cols(a, pair, lane):
    h0 = 2 * pair
    return jnp.where(lane < SSD_HEADDIM, a[:, h0:h0 + 1], a[:, h0 + 1:h0 + 2])


X_TILES = D_INNER // LANES
B_TILE0 = X_TILES
C_TILE0 = X_TILES + BC_DIM // LANES
XBC_TILES = CONV_DIM // LANES


def _ssd_intra(xbc_ref, dt_raw, par_ref, tri_ref, seg_ref, y_ref, xddt_ref):
    dt_bias, a_log = par_ref[0:1, 0:LANES], par_ref[1:2, 0:LANES]
    pre = dt_raw + dt_bias
    dt = jnp.maximum(pre, 0.0) + jnp.log(1.0 + jnp.exp(-jnp.abs(pre)))
    d_a = dt * -jnp.exp(a_log)
    tri = tri_ref[...]
    acum = jnp.dot(tri, d_a, precision=HIGHEST, preferred_element_type=F32)
    a_end = jnp.dot(seg_ref[...], d_a, precision=HIGHEST, preferred_element_type=F32)
    acum_t = acum.T
    causal = tri > 0.5
    e_acum = jnp.exp(acum)
    e_end = jnp.exp(a_end)
    d_end = jnp.exp(a_end - acum)
    lane = lax.broadcasted_iota(jnp.int32, (CHUNK, LANES), 1)
    for grp in range(SSD_GROUPS):
        cb = _dot_nt(xbc_ref[C_TILE0 + grp].astype(BF16), xbc_ref[B_TILE0 + grp].astype(BF16))
        for half in range(2):
            pair = 2 * grp + half
            plo = pair * LANES
            xs = xbc_ref[pair]
            xd = xs * _pair_cols(dt, pair, lane)
            xd16 = xd.astype(BF16)
            yd = []
            for k in range(2):
                h = 2 * pair + k
                decay = jnp.exp(jnp.where(causal, acum[:, h:h + 1] - acum_t[h:h + 1, :], -jnp.inf))
                yd.append(_dot((cb * decay).astype(BF16), xd16))
            y_ref[:, plo:plo + LANES] = jnp.where(lane < SSD_HEADDIM, yd[0], yd[1]) + par_ref[2:3, plo:plo + LANES] * xs
            xddt_ref[plo:plo + LANES, :] = (xd * _pair_cols(d_end, pair, lane)).T
    return e_acum, e_end


def _ssd_finish(y_ref, z_ref, par_ref, o_ref):
    o_ref[...] = _rms(y_ref[...] * _silu(z_ref[...].astype(F32)), par_ref[3:4, :]).astype(o_ref.dtype)


def _ssd_prompt_body(z_ref, x_ref, b_ref, c_ref, dt_ref, cw_ref, par_ref, tri_ref, seg_ref,
                     o_ref, st_ref, ext_ref, xbc_ref, y_ref, xddt_ref):
    first = pl.program_id(1) == 0

    @pl.when(first)
    def _():
        ext_ref[:, 0:8, :] = jnp.zeros((XBC_TILES, 8, LANES), F32)
        st_ref[...] = jnp.zeros_like(st_ref)

    for t in range(XBC_TILES):
        lo = t * LANES
        if t < B_TILE0:
            ext_ref[t, 8:8 + CHUNK, :] = x_ref[:, lo:lo + LANES]
        elif t < C_TILE0:
            ext_ref[t, 8:8 + CHUNK, :] = b_ref[:, lo - D_INNER:lo - D_INNER + LANES]
        else:
            ext_ref[t, 8:8 + CHUNK, :] = c_ref[:, lo - D_INNER - BC_DIM:lo - D_INNER - BC_DIM + LANES]

    for t in range(XBC_TILES):
        acc = cw_ref[t, CONV_W:CONV_W + 1, :] + ext_ref[t, 5:5 + CHUNK, :] * cw_ref[t, 0:1, :]
        for k in range(1, CONV_W):
            acc = acc + ext_ref[t, 5 + k:5 + k + CHUNK, :] * cw_ref[t, k:k + 1, :]
        xbc_ref[t] = _silu(acc)
    ext_ref[:, 0:8, :] = ext_ref[:, CHUNK:CHUNK + 8, :]

    e_acum, e_end = _ssd_intra(xbc_ref, dt_ref[...], par_ref, tri_ref, seg_ref, y_ref, xddt_ref)

    lane = lax.broadcasted_iota(jnp.int32, (CHUNK, LANES), 1)
    gw = SSD_STATE * 2
    for grp in range(SSD_GROUPS):
        rows = slice(grp * gw, (grp + 1) * gw)
        prev = st_ref[rows, :]
        y_off = _dot_nt(xbc_ref[C_TILE0 + grp].astype(BF16), prev.astype(BF16))
        for half in range(2):
            pair = 2 * grp + half
            plo = pair * LANES
            y_ref[:, plo:plo + LANES] += y_off[:, half * LANES:(half + 1) * LANES] * _pair_cols(e_acum, pair, lane)
        new = _dot(xddt_ref[rows, :].astype(BF16), xbc_ref[B_TILE0 + grp].astype(BF16))
        for k in range(4):
            h = 4 * grp + k
            hr = slice(grp * gw + k * SSD_HEADDIM, grp * gw + (k + 1) * SSD_HEADDIM)
            dec = jnp.broadcast_to(jnp.broadcast_to(e_end[:, h:h + 1], (CHUNK, SSD_STATE))[0:1, :],
                                   (SSD_HEADDIM, SSD_STATE))
            st_ref[hr, :] = prev[k * SSD_HEADDIM:(k + 1) * SSD_HEADDIM, :] * dec + new[k * SSD_HEADDIM:(k + 1) * SSD_HEADDIM, :]

    _ssd_finish(y_ref, z_ref, par_ref, o_ref)


def _ssd_consts(segment):
    r = jnp.arange(CHUNK)
    same = (r[:, None] // segment) == (r[None, :] // segment)
    return (same & (r[None, :] <= r[:, None])).astype(F32), same.astype(F32)


def _ssd_prompt(u_zq, u_xd, conv_wb, par, b, s):
    assert s % CHUNK == 0
    tri, seg = _ssd_consts(CHUNK)
    zq3 = u_zq.reshape(b, s, U_ZQ)
    xd3 = u_xd.reshape(b, s, U_XD)
    col = lambda width, blk: pl.BlockSpec((None, CHUNK, width), lambda i, c: (i, c, blk))
    y, st = pl.pallas_call(
        _ssd_prompt_body,
        grid=(b, s // CHUNK),
        in_specs=[col(D_INNER, 0), col(D_INNER, 0), col(BC_DIM, D_INNER // BC_DIM),
                  col(BC_DIM, D_INNER // BC_DIM + 1), col(LANES, CONV_DIM // LANES),
                  _const_spec((XBC_TILES, 8, LANES)), _const_spec((4, D_INNER)),
                  _const_spec((CHUNK, CHUNK)), _const_spec((CHUNK, CHUNK))],
        out_specs=[pl.BlockSpec((None, CHUNK, D_INNER), lambda i, c: (i, c, 0)),
                   pl.BlockSpec((None, D_INNER, SSD_STATE), lambda i, c: (i, 0, 0))],
        out_shape=[jax.ShapeDtypeStruct((b, s, D_INNER), BF16),
                   jax.ShapeDtypeStruct((b, D_INNER, SSD_STATE), F32)],
        scratch_shapes=[pltpu.VMEM((XBC_TILES, CHUNK + 8, LANES), F32), pltpu.VMEM((XBC_TILES, CHUNK, LANES), F32),
                        pltpu.VMEM((CHUNK, D_INNER), F32), pltpu.VMEM((D_INNER, CHUNK), F32)],
        compiler_params=_params("parallel", "arbitrary"),
        name="ssd_prompt",
    )(zq3, xd3, xd3, xd3, xd3, conv_wb, par, tri, seg)
    return y.reshape(b * s, D_INNER), st


def _conv_sample_body(seq_ref, cw_ref, o_ref, *, tlen):
    acc = cw_ref[4:5, :] + seq_ref[:, 0:tlen, :] * cw_ref[0:1, :]
    for k in range(1, CONV_W):
        acc = acc + seq_ref[:, k:k + tlen, :] * cw_ref[k:k + 1, :]
    o_ref[...] = _silu(acc)


def _conv_sample(seq, conv_wb, tlen):
    b = seq.shape[0]
    bt = 8
    return pl.pallas_call(
        functools.partial(_conv_sample_body, tlen=tlen),
        grid=(b // bt,),
        in_specs=[pl.BlockSpec((bt, CONV_W - 1 + tlen, CONV_DIM), lambda i: (i, 0, 0)),
                  _const_spec((8, CONV_DIM))],
        out_specs=pl.BlockSpec((bt, tlen, CONV_DIM), lambda i: (i, 0, 0)),
        out_shape=jax.ShapeDtypeStruct((b, tlen, CONV_DIM), F32),
        compiler_params=_params("parallel"),
        name="conv_sample",
    )(seq, conv_wb)


SSD_BT = 8


def _ssd_sample_body(z_ref, xbc_in_ref, dt_ref, par_ref, tri_ref, seg_ref, st_in_ref,
                     o_ref, st_out_ref, xbc_ref, dtp_ref, zp_ref, y_ref, xddt_ref, op_ref, *, tlen):
    rows = SSD_BT * tlen
    xbc_ref[...] = jnp.zeros_like(xbc_ref)
    dtp_ref[...] = jnp.zeros_like(dtp_ref)
    zp_ref[...] = jnp.zeros_like(zp_ref)
    for t in range(XBC_TILES):
        xbc_ref[t, 0:rows, :] = xbc_in_ref[:, t * LANES:(t + 1) * LANES]
    dtp_ref[0:rows, :] = dt_ref[...]
    zp_ref[0:rows, :] = z_ref[...].astype(F32)
    e_acum, e_end = _ssd_intra(xbc_ref, dtp_ref[...], par_ref, tri_ref, seg_ref, y_ref, xddt_ref)

    lane = lax.broadcasted_iota(jnp.int32, (CHUNK, LANES), 1)
    row_b = lax.broadcasted_iota(jnp.int32, (CHUNK, 1), 0) // tlen
    win = 16
    per_win = win // tlen
    win_b = lax.broadcasted_iota(jnp.int32, (win, 1), 0) // tlen
    gw = SSD_STATE * 2
    e_end_b = [jnp.broadcast_to(e_end[:, h:h + 1], (CHUNK, SSD_STATE)) for h in range(SSD_HEADS)]
    for grp in range(SSD_GROUPS):
        rsl = slice(grp * gw, (grp + 1) * gw)
        c16 = xbc_ref[C_TILE0 + grp].astype(BF16)
        b32 = xbc_ref[B_TILE0 + grp]
        xddt16 = xddt_ref[rsl, :].astype(BF16)
        for w in range(rows // win):
            c_win = c16[w * win:(w + 1) * win, :]
            y_off = jnp.zeros((win, gw), F32)
            for k in range(per_win):
                bb = w * per_win + k
                prev = st_in_ref[bb, rsl, :]
                y_off = jnp.where(win_b == k, _dot_nt(c_win, prev.astype(BF16)), y_off)
                new = _dot(xddt16, jnp.where(row_b == bb, b32, 0.0).astype(BF16))
                for hh in range(4):
                    h = 4 * grp + hh
                    hs = slice(hh * SSD_HEADDIM, (hh + 1) * SSD_HEADDIM)
                    dec = jnp.broadcast_to(e_end_b[h][bb * tlen:bb * tlen + 1, :], (SSD_HEADDIM, SSD_STATE))
                    st_out_ref[bb, grp * gw + hh * SSD_HEADDIM:grp * gw + (hh + 1) * SSD_HEADDIM, :] = (
                        prev[hs, :] * dec + new[hs, :])
            for half in range(2):
                pair = 2 * grp + half
                plo = pair * LANES
                scale = _pair_cols(e_acum, pair, lane)[w * win:(w + 1) * win, :]
                y_ref[w * win:(w + 1) * win, plo:plo + LANES] += y_off[:, half * LANES:(half + 1) * LANES] * scale

    _ssd_finish(y_ref, zp_ref, par_ref, op_ref)
    o_ref[...] = op_ref[0:rows, :].astype(o_ref.dtype)


def _ssd_sample(u_zq, xbc, u_xd, par, state, layer, b, tlen):
    rows = SSD_BT * tlen
    assert CHUNK % rows == 0 and 16 % tlen == 0 and rows % 16 == 0
    tri, seg = _ssd_consts(tlen)
    row = lambda width, blk: pl.BlockSpec((rows, width), lambda i: (i, blk))
    st_spec = pl.BlockSpec((SSD_BT, D_INNER, SSD_STATE), lambda i: (i, 0, 0))
    st_in_spec = pl.BlockSpec((None, SSD_BT, D_INNER, SSD_STATE), lambda i: (layer, i, 0, 0))
    y, st = pl.pallas_call(
        functools.partial(_ssd_sample_body, tlen=tlen),
        grid=(b // SSD_BT,),
        in_specs=[row(D_INNER, 0), row(CONV_DIM, 0), row(LANES, CONV_DIM // LANES), _const_spec((4, D_INNER)),
                  _const_spec((CHUNK, CHUNK)), _const_spec((CHUNK, CHUNK)), st_in_spec],
        out_specs=[row(D_INNER, 0), st_spec],
        out_shape=[jax.ShapeDtypeStruct((b * tlen, D_INNER), BF16),
                   jax.ShapeDtypeStruct((b, D_INNER, SSD_STATE), F32)],
        scratch_shapes=[pltpu.VMEM((XBC_TILES, CHUNK, LANES), F32), pltpu.VMEM((CHUNK, LANES), F32),
                        pltpu.VMEM((CHUNK, D_INNER), F32), pltpu.VMEM((CHUNK, D_INNER), F32),
                        pltpu.VMEM((D_INNER, CHUNK), F32), pltpu.VMEM((CHUNK, D_INNER), F32)],
        compiler_params=_params("parallel"),
        name="ssd_sample",
    )(u_zq, xbc, u_xd, par, tri, seg, state)
    return y, st


def _prep_weights(norm_g, w_ffn_gu, w_ffn_down, w_in_a, conv_w, conv_b, dt_bias, a_log, d_skip, ssd_norm_g,
                  w_out_a, w_in_b, w_out_b, kv_norm_g, w_kv_shared, mem_norm_g, w_mem_kv, final_norm_g):
    p = {}
    p["g_ffn"] = [[jnp.stack([norm_g[l, 0 + 4 * k], norm_g[l, 1 + 4 * k], final_norm_g]) for k in range(2)]
                  for l in range(DEPTH)]
    p["w_gu"] = w_ffn_gu.astype(BF16)
    p["w_down"] = w_ffn_down.astype(BF16)
    p["w_zq"] = jnp.concatenate([w_in_a[:, :, :D_INNER], w_in_a[:, :, SSD_IN:]], axis=-1).astype(BF16)
    w_dt = jnp.pad(w_in_a[:, :, SSD_MAIN:SSD_IN], ((0, 0), (0, 0), (0, LANES - SSD_HEADS)))
    p["w_xd"] = jnp.concatenate([w_in_a[:, :, D_INNER:SSD_MAIN], w_dt], axis=-1).astype(BF16)
    p["g_in_a"] = norm_g[:N_A, 2]
    p["g_in_b"] = norm_g[N_A:, 2]
    p["conv_wb"] = jnp.concatenate([conv_w, conv_b[:, None, :], jnp.zeros((N_A, 3, CONV_DIM), F32)], axis=1)
    p["conv_wb_tiles"] = p["conv_wb"].reshape(N_A, 8, XBC_TILES, LANES).transpose(0, 2, 1, 3)
    lane_pad = lambda v: jnp.pad(v, ((0, 0), (0, D_INNER - v.shape[1])))
    p["ssd_par"] = jnp.stack([lane_pad(dt_bias), lane_pad(a_log), jnp.repeat(d_skip, SSD_HEADDIM, axis=1),
                              ssd_norm_g], axis=1)
    p["w_out_a"] = w_out_a.astype(BF16)
    q_dil = N_DIL * DIL_GW
    p["w_in_b"] = jnp.concatenate([w_in_b[:, :, q_dil:], w_in_b[:, :, :q_dil]], axis=-1).astype(BF16)
    p["w_out_b"] = w_out_b.astype(BF16)
    p["w_kv"] = w_kv_shared.astype(BF16)
    p["w_mem_kv"] = w_mem_kv.astype(BF16)
    n = N_DIL * DIL_QH
    p["slopes"] = jnp.exp2(-8.0 * jnp.arange(1, n + 1, dtype=F32) / n)
    p["norm_g"] = norm_g
    p["kv_norm_g"] = kv_norm_g
    p["mem_norm_g"] = mem_norm_g
    return p


def _trunk(p, x, b, s, mem_kv, conv_prev, ssm_prev, win_bufs):
    prompt = win_bufs is None
    ssm_new, conv_new = [], []
    kv = None
    for l in range(DEPTH):
        g = p["norm_g"][l]
        x = _ffn(x, p["g_ffn"][l][0], p["w_gu"], p["w_down"], l, 0)
        if l < N_A:
            u_zq = _norm_matmul(x, p["g_in_a"], p["w_zq"], layers=(l,), out_dtype=BF16)[0]
            u_xd = _norm_matmul(x, p["g_in_a"], p["w_xd"], layers=(l,))[0]
            raw = u_xd.reshape(b, s, U_XD)
            if prompt:
                y_mix, st = _ssd_prompt(u_zq, u_xd, p["conv_wb_tiles"][l], p["ssd_par"][l], b, s)
                conv_new.append(raw[:, s - (CONV_W - 1):, :CONV_DIM])
                mem_o = _mem_attn_prompt(u_zq, D_INNER // MEM_Q, mem_kv, l, b, s)
            else:
                seq = jnp.concatenate([conv_prev[l], raw[:, :, :CONV_DIM]], axis=1)
                xbc = _conv_sample(seq, p["conv_wb"][l], s).reshape(b * s, CONV_DIM)
                y_mix, st = _ssd_sample(u_zq, xbc, u_xd, p["ssd_par"][l], ssm_prev, l, b, s)
                conv_new.append(seq[:, s:])
                mem_o = _mem_attn_sample(u_zq, D_INNER // MEM_Q, mem_kv, l, b, s)
            ssm_new.append(st.reshape(b, SSD_HEADS, SSD_HEADDIM, SSD_STATE))
            x = _out_proj_a(y_mix, mem_o, x, g[3], p["w_out_a"], l)
        else:
            i = l - N_A
            u = _norm_matmul(x, p["g_in_b"], p["w_in_b"], layers=(i,))[0]
            outs, lses = [], []
            for gi in range(N_DIL):
                if prompt:
                    o, lse = _dil_attn_prompt(p["slopes"], u, kv, gi, b, s)
                else:
                    o, lse = _dil_attn_sample(p["slopes"], u, kv, win_bufs[gi], gi, b, s)
                outs.append(o)
                lses.append(lse)
            if prompt:
                mem_o = _mem_attn_prompt(u, 0, mem_kv, l, b, s)
            else:
                mem_o = _mem_attn_sample(u, 0, mem_kv, l, b, s)
            x = _out_proj_b(outs, lses, mem_o, x, g[3], p["w_out_b"], i)
        x = _ffn(x, p["g_ffn"][l][1], p["w_gu"], p["w_down"], l, 1, final=(l == DEPTH - 1))
        if l == N_A - 1:
            kv = _norm_matmul(x, p["kv_norm_g"][None], p["w_kv"][None])[0]
    return x, jnp.stack(conv_new), jnp.stack(ssm_new), kv


def kernel(x_prompt, x_sample, mem_prompt, cache_mem_kv, state_ssm, state_conv, cache_win_g1, cache_win_g2,
           cache_win_g3, norm_g, w_ffn_gu, w_ffn_down, w_in_a, conv_w, conv_b, dt_bias, a_log, d_skip,
           ssd_norm_g, w_out_a, w_in_b, w_out_b, kv_norm_g, w_kv_shared, mem_norm_g, w_mem_kv, final_norm_g):
    p = _prep_weights(norm_g, w_ffn_gu, w_ffn_down, w_in_a, conv_w, conv_b, dt_bias, a_log, d_skip, ssd_norm_g,
                      w_out_a, w_in_b, w_out_b, kv_norm_g, w_kv_shared, mem_norm_g, w_mem_kv, final_norm_g)
    bp, sp = x_prompt.shape[:2]
    bs, ss = x_sample.shape[:2]

    mem_flat = mem_prompt.reshape(bp * N_MEM, D_MODEL)
    mem_kv_p = _norm_matmul(mem_flat, mem_norm_g, p["w_mem_kv"]).reshape(DEPTH, bp, N_MEM, 2 * MEM_Q)
    y_p, conv_p, ssm_p, kv_p = _trunk(p, x_prompt.reshape(bp * sp, D_MODEL), bp, sp, mem_kv_p, None, None, None)
    kv_p = kv_p.reshape(bp, sp, KV_W)
    win_p = [kv_p[:, sp - min(w, sp):, gi * DIL_GW:(gi + 1) * DIL_GW].reshape(bp, min(w, sp), 2, DIL_KVH, DIL_HD)
             for gi, (w, _) in enumerate(DIL_CONFIGS)]

    y_s, conv_s, ssm_s, kv_s = _trunk(p, x_sample.reshape(bs * ss, D_MODEL), bs, ss, cache_mem_kv, state_conv,
                                      state_ssm.reshape(N_A, bs, D_INNER, SSD_STATE),
                                      (cache_win_g1, cache_win_g2, cache_win_g3))
    kv_s = kv_s.reshape(bs, ss, N_DIL, 2, DIL_KVH, DIL_HD)
    return (y_p.reshape(bp, sp, D_MODEL), y_s.reshape(bs, ss, D_MODEL),
            mem_kv_p.reshape(DEPTH, bp, N_MEM, 2, MEM_HEADS, MEM_HD), ssm_p, conv_p, win_p[0], win_p[1], win_p[2],
            ssm_s, conv_s, kv_s[:, :, 0], kv_s[:, :, 1], kv_s[:, :, 2])
```

```python
import functools

import jax
import jax.numpy as jnp
from jax import lax
from jax.experimental import pallas as pl
from jax.experimental.pallas import tpu as pltpu

F32 = jnp.float32
BF16 = jnp.bfloat16
HIGHEST = lax.Precision.HIGHEST

D_MODEL = 1024
DEPTH = 4
N_A = 2
D_FF = 2816
D_INNER = 2048
SSD_HEADDIM = 64
SSD_HEADS = 32
SSD_GROUPS = 8
SSD_STATE = 128
CONV_W = 4
CONV_DIM = D_INNER + 2 * SSD_GROUPS * SSD_STATE
BC_DIM = SSD_GROUPS * SSD_STATE
DIL_CONFIGS = ((128, 1), (512, 4), (2048, 16))
N_DIL = 3
DIL_QH = 4
DIL_KVH = 2
DIL_HD = 128
DIL_NK = 128
DIL_GW = DIL_QH * DIL_HD
N_MEM = 256
MEM_HEADS = 4
MEM_HD = 256
MEM_Q = MEM_HEADS * MEM_HD
EPS = 1e-6
SSD_MAIN = 2 * D_INNER + 2 * BC_DIM
SSD_IN = SSD_MAIN + SSD_HEADS
U_ZQ = D_INNER + MEM_Q
U_XD = CONV_DIM + 128
U_B = MEM_Q + N_DIL * DIL_GW
KV_W = N_DIL * 2 * DIL_KVH * DIL_HD

LANES = 128
CHUNK = 128
FF_CHUNK = 256
VMEM_LIMIT = 56 * 2 ** 20


def _params(*sem):
    return pltpu.CompilerParams(dimension_semantics=sem, vmem_limit_bytes=VMEM_LIMIT)


def _const_spec(shape):
    return pl.BlockSpec(shape, lambda *_: (0,) * len(shape), pipeline_mode=pl.Buffered(1))


def _rms(x, g):
    return x * lax.rsqrt(jnp.mean(x * x, axis=-1, keepdims=True) + EPS) * g


def _silu(x):
    return x / (1.0 + jnp.exp(-x))


def _dot(a, b):
    return jnp.dot(a, b, preferred_element_type=F32)


def _dot_nt(a, b):
    return lax.dot_general(a, b, (((1,), (1,)), ((), ())), preferred_element_type=F32)


def _row_tile(t, pref):
    return pref if t % pref == 0 else t


def _ffn_body(x_ref, g_ref, wgu_ref, wd_ref, o_ref, acc_ref, *, final):
    x = x_ref[...]
    h = _rms(x, g_ref[0:1, :]).astype(BF16)
    for c in range(D_FF // FF_CHUNK):
        lo = c * FF_CHUNK
        gate = _dot(h, wgu_ref[:, lo:lo + FF_CHUNK])
        up = _dot(h, wgu_ref[:, D_FF + lo:D_FF + lo + FF_CHUNK])
        part = _dot((_silu(gate) * up).astype(BF16), wd_ref[lo:lo + FF_CHUNK, :])
        if c == 0:
            acc_ref[...] = part
        else:
            acc_ref[...] += part
    y = x + 0.5 * _rms(acc_ref[...], g_ref[1:2, :])
    if final:
        y = _rms(y, g_ref[2:3, :])
    o_ref[...] = y


def _pick_spec(shape, *lead):
    return pl.BlockSpec((None,) * len(lead) + tuple(shape), lambda *_: tuple(lead) + (0,) * len(shape),
                        pipeline_mode=pl.Buffered(1))


def _ffn(x, g3, wgu, wd, layer, which, final=False):
    t = x.shape[0]
    tm = _row_tile(t, 512)
    return pl.pallas_call(
        functools.partial(_ffn_body, final=final),
        grid=(t // tm,),
        in_specs=[pl.BlockSpec((tm, D_MODEL), lambda i: (i, 0)),
                  _const_spec((3, D_MODEL)),
                  _pick_spec((D_MODEL, 2 * D_FF), layer, which),
                  _pick_spec((D_FF, D_MODEL), layer, which)],
        out_specs=pl.BlockSpec((tm, D_MODEL), lambda i: (i, 0)),
        out_shape=jax.ShapeDtypeStruct((t, D_MODEL), F32),
        scratch_shapes=[pltpu.VMEM((tm, D_MODEL), F32)],
        compiler_params=_params("parallel"),
        name="ffn",
    )(x, g3, wgu, wd)


def _nmm_body(x_ref, g_ref, w_ref, o_ref, h_ref):
    @pl.when(pl.program_id(2) == 0)
    def _():
        h_ref[...] = _rms(x_ref[...], g_ref[...]).astype(BF16)

    o_ref[...] = _dot(h_ref[...], w_ref[...]).astype(o_ref.dtype)


NMM_WHOLE_N = 4608


def _nmm_tiles(t, n):
    if n <= NMM_WHOLE_N:
        return _row_tile(t, 512 if n > 2560 else 1024), n
    return _row_tile(t, 1024), max(c for c in range(LANES, 1536 + 1, LANES) if n % c == 0)


def _norm_matmul(x, g, w, layers=None, out_dtype=F32):
    t, n = x.shape[0], w.shape[2]
    layers = tuple(range(w.shape[0])) if layers is None else tuple(layers)
    first, count = layers[0], len(layers)
    assert layers == tuple(range(first, first + count))
    tm, tn = _nmm_tiles(t, n)
    w_mode = dict(pipeline_mode=pl.Buffered(1)) if (tn == n and count == 1) else {}
    return pl.pallas_call(
        _nmm_body,
        grid=(count, t // tm, n // tn),
        in_specs=[pl.BlockSpec((tm, D_MODEL), lambda l, i, j: (i, 0)),
                  pl.BlockSpec((None, 1, D_MODEL), lambda l, i, j: (first + l, 0, 0)),
                  pl.BlockSpec((None, D_MODEL, tn), lambda l, i, j: (first + l, 0, j), **w_mode)],
        out_specs=pl.BlockSpec((None, tm, tn), lambda l, i, j: (l, i, j)),
        out_shape=jax.ShapeDtypeStruct((count, t, n), out_dtype),
        scratch_shapes=[pltpu.VMEM((tm, D_MODEL), BF16)],
        compiler_params=_params("parallel", "parallel", "arbitrary"),
        name="norm_matmul",
    )(x, g.reshape(g.shape[0], 1, D_MODEL), w)


def _out_a_body(y_ref, m_ref, x_ref, g_ref, w_ref, o_ref):
    acc = _dot(y_ref[...], w_ref[0:D_INNER, :])
    acc += _dot(m_ref[...], w_ref[D_INNER:D_INNER + MEM_Q, :])
    o_ref[...] = x_ref[...] + _rms(acc, g_ref[...])


def _out_proj_a(y, m, x, g, w, layer):
    t = x.shape[0]
    tm = _row_tile(t, 512)
    row = lambda width: pl.BlockSpec((tm, width), lambda i: (i, 0))
    return pl.pallas_call(
        _out_a_body,
        grid=(t // tm,),
        in_specs=[row(D_INNER), row(MEM_Q), row(D_MODEL), _const_spec((1, D_MODEL)),
                  _pick_spec((D_INNER + MEM_Q, D_MODEL), layer)],
        out_specs=row(D_MODEL),
        out_shape=jax.ShapeDtypeStruct((t, D_MODEL), F32),
        compiler_params=_params("parallel"),
        name="out_proj_a",
    )(y, m, x, g.reshape(1, D_MODEL), w)


def _out_b_body(o1_ref, o2_ref, o3_ref, l1_ref, l2_ref, l3_ref, m_ref, x_ref, g_ref, w_ref, o_ref):
    l1, l2, l3 = l1_ref[...], l2_ref[...], l3_ref[...]
    mx = jnp.maximum(jnp.maximum(l1, l2), l3)
    e1, e2, e3 = jnp.exp(l1 - mx), jnp.exp(l2 - mx), jnp.exp(l3 - mx)
    den = e1 + e2 + e3
    a1, a2, a3 = e1 / den, e2 / den, e3 / den
    mix = []
    for h in range(DIL_QH):
        c = slice(h * DIL_HD, (h + 1) * DIL_HD)
        mix.append(a1[:, h:h + 1] * o1_ref[:, c] + a2[:, h:h + 1] * o2_ref[:, c] + a3[:, h:h + 1] * o3_ref[:, c])
    acc = _dot(jnp.concatenate(mix, axis=1).astype(BF16), w_ref[0:DIL_GW, :])
    acc += _dot(m_ref[...], w_ref[DIL_GW:DIL_GW + MEM_Q, :])
    o_ref[...] = x_ref[...] + _rms(acc, g_ref[...])


def _out_proj_b(outs, lses, m, x, g, w, layer):
    t = x.shape[0]
    tm = _row_tile(t, 512)
    row = lambda width: pl.BlockSpec((tm, width), lambda i: (i, 0))
    return pl.pallas_call(
        _out_b_body,
        grid=(t // tm,),
        in_specs=[row(DIL_GW)] * 3 + [row(LANES)] * 3 + [row(MEM_Q), row(D_MODEL), _const_spec((1, D_MODEL)),
                                                         _pick_spec((DIL_GW + MEM_Q, D_MODEL), layer)],
        out_specs=row(D_MODEL),
        out_shape=jax.ShapeDtypeStruct((t, D_MODEL), F32),
        compiler_params=_params("parallel"),
        name="out_proj_b",
    )(*outs, *lses, m, x, g.reshape(1, D_MODEL), w)


def _softmax_rows(s):
    e = jnp.exp(s - jnp.max(s, axis=-1, keepdims=True))
    return e / jnp.sum(e, axis=-1, keepdims=True)


def _mem_prompt_body(q_ref, kv_ref, o_ref):
    cols = [slice(h * MEM_HD, (h + 1) * MEM_HD) for h in range(MEM_HEADS)]
    scores = [_dot_nt(q_ref[:, c].astype(BF16), kv_ref[:, c].astype(BF16)) * (MEM_HD ** -0.5) for c in cols]
    probs = [_softmax_rows(s).astype(BF16) for s in scores]
    for h, c in enumerate(cols):
        v = kv_ref[:, MEM_Q + h * MEM_HD:MEM_Q + (h + 1) * MEM_HD].astype(BF16)
        o_ref[:, c] = _dot(probs[h], v).astype(o_ref.dtype)


def _mem_attn_prompt(u, q_block, mem_kv, layer, b, s):
    c = u.shape[1]
    tq = _row_tile(s, 1024)
    out = pl.pallas_call(
        _mem_prompt_body,
        grid=(b, s // tq),
        in_specs=[pl.BlockSpec((None, tq, MEM_Q), lambda i, j: (i, j, q_block)),
                  pl.BlockSpec((None, None, N_MEM, 2 * MEM_Q), lambda i, j: (layer, i, 0, 0))],
        out_specs=pl.BlockSpec((None, tq, MEM_Q), lambda i, j: (i, j, 0)),
        out_shape=jax.ShapeDtypeStruct((b, s, MEM_Q), BF16),
        compiler_params=_params("parallel", "parallel"),
        name="mem_attn_prompt",
    )(u.reshape(b, s, c), mem_kv)
    return out.reshape(b * s, MEM_Q)


MEM_BT = 4
MEM_LT = MEM_HD // LANES
MEM_ROWS = 2 * MEM_LT * MEM_HEADS


def _mem_rows(kv_ref, bb, kv, h):
    parts = [kv_ref[bb, pl.ds((kv * MEM_LT + lt) * MEM_HEADS + h, N_MEM, stride=MEM_ROWS), :]
             for lt in range(MEM_LT)]
    return jnp.concatenate(parts, axis=1)


def _mem_sample_body(q_ref, kv_ref, o_ref, *, tlen):
    rows = MEM_BT * tlen
    owner = lax.broadcasted_iota(jnp.int32, (rows, 1), 0) // tlen
    q_all = q_ref[...]
    scores = []
    for h in range(MEM_HEADS):
        q = q_all[:, h * MEM_HD:(h + 1) * MEM_HD].astype(BF16)
        s = jnp.zeros((rows, N_MEM), F32)
        for bb in range(MEM_BT):
            k = _mem_rows(kv_ref, bb, 0, h).astype(BF16)
            s = jnp.where(owner == bb, _dot_nt(q, k), s)
        scores.append(s)
    probs = [_softmax_rows(s * (MEM_HD ** -0.5)).astype(BF16) for s in scores]
    for h in range(MEM_HEADS):
        o = jnp.zeros((rows, MEM_HD), F32)
        for bb in range(MEM_BT):
            v = _mem_rows(kv_ref, bb, 1, h).astype(BF16)
            o = jnp.where(owner == bb, _dot(probs[h], v), o)
        o_ref[:, h * MEM_HD:(h + 1) * MEM_HD] = o.astype(o_ref.dtype)


def _mem_attn_sample(u, q_block, mem_kv, layer, b, tlen):
    rows = MEM_BT * tlen
    depth = mem_kv.shape[0]
    mem_kv = mem_kv.reshape(depth, b, N_MEM, 2, MEM_HEADS, MEM_LT, LANES).transpose(0, 1, 2, 3, 5, 4, 6)
    mem_kv = mem_kv.reshape(depth, b, N_MEM * MEM_ROWS, LANES)
    return pl.pallas_call(
        functools.partial(_mem_sample_body, tlen=tlen),
        grid=(b // MEM_BT,),
        in_specs=[pl.BlockSpec((rows, MEM_Q), lambda i: (i, q_block)),
                  pl.BlockSpec((None, MEM_BT, N_MEM * MEM_ROWS, LANES), lambda i: (layer, i, 0, 0))],
        out_specs=pl.BlockSpec((rows, MEM_Q), lambda i: (i, 0)),
        out_shape=jax.ShapeDtypeStruct((b * tlen, MEM_Q), BF16),
        compiler_params=_params("parallel"),
        name="mem_attn_sample",
    )(u, mem_kv)


DIL_TB = 2048


def _dil_prompt_body(sl_ref, q_ref, kp_ref, vp_ref, kc_ref, vc_ref, o_ref, l_ref, *, gi, dil):
    span = DIL_NK * dil
    has_prev = pl.program_id(1) > 0
    head = pl.program_id(2)
    qi = lax.broadcasted_iota(jnp.int32, (DIL_NK, DIL_NK), 0)
    kj = lax.broadcasted_iota(jnp.int32, (DIL_NK, DIL_NK), 1)
    du_prev = DIL_NK + qi - kj
    du_own = qi - kj
    valid_prev = du_prev <= DIL_NK
    valid_own = du_own >= 0
    my_lane = kj == head
    neg_slope = -sl_ref[gi * DIL_QH + head]

    @pl.when(head == 0)
    def _():
        l_ref[...] = jnp.zeros_like(l_ref)

    bias_prev = neg_slope * (dil * du_prev).astype(F32)
    bias_own = neg_slope * (dil * du_own).astype(F32)
    scale = DIL_HD ** -0.5

    def rows(start):
        return pl.ds(start, DIL_NK) if dil == 1 else pl.ds(start, DIL_NK, stride=dil)

    def scores(blk, r):
        own = rows(blk * span + r)
        if blk == 0:
            k_prev, v_prev, ok_prev = kp_ref[rows(r), :], vp_ref[rows(r), :], valid_prev & has_prev
        else:
            prev = rows((blk - 1) * span + r)
            k_prev, v_prev, ok_prev = kc_ref[prev, :], vc_ref[prev, :], valid_prev
        q = q_ref[own, :].astype(BF16)
        s_prev = jnp.where(ok_prev, _dot_nt(q, k_prev.astype(BF16)) * scale + bias_prev, -jnp.inf)
        s_own = jnp.where(valid_own, _dot_nt(q, kc_ref[own, :].astype(BF16)) * scale + bias_own, -jnp.inf)
        mx = jnp.maximum(jnp.max(s_prev, axis=-1, keepdims=True), jnp.max(s_own, axis=-1, keepdims=True))
        return own, s_prev, s_own, mx, v_prev

    def weights(own, s_prev, s_own, mx, v_prev):
        e_prev = jnp.exp(s_prev - mx)
        e_own = jnp.exp(s_own - mx)
        den = jnp.sum(e_prev, axis=-1, keepdims=True) + jnp.sum(e_own, axis=-1, keepdims=True)
        return own, (e_prev / den).astype(BF16), (e_own / den).astype(BF16), mx + jnp.log(den), v_prev

    def finish(own, p_prev, p_own, lse, v_prev):
        o_ref[own, :] = _dot(p_prev, v_prev.astype(BF16)) + _dot(p_own, vc_ref[own, :].astype(BF16))
        l_ref[own, :] = jnp.where(my_lane, lse, l_ref[own, :])

    def streams(group):
        for st in [weights(*sc) for sc in [scores(blk, r) for blk, r in group]]:
            finish(*st)

    together = 8
    if dil <= together:
        todo = [(blk, r) for blk in range(DIL_TB // span) for r in range(dil)]
        for g0 in range(0, len(todo), together):
            streams(todo[g0:g0 + together])
    else:
        for blk in range(DIL_TB // span):
            def per_group(g, carry, blk=blk):
                streams([(blk, g * together + k) for k in range(together)])
                return carry

            lax.fori_loop(0, dil // together, per_group, 0)


def _dil_attn_prompt(slopes, u, kv, gi, b, s):
    window, dil = DIL_CONFIGS[gi]
    span = DIL_NK * dil
    assert window // dil == DIL_NK and s % DIL_TB == 0 and DIL_TB % span == 0
    per_tb = DIL_TB // span
    qpk = DIL_QH // DIL_KVH
    q0 = (MEM_Q + gi * DIL_GW) // DIL_HD
    k0 = gi * DIL_GW // DIL_HD
    v0 = k0 + DIL_KVH
    blk = lambda imap: pl.BlockSpec((None, DIL_TB, DIL_HD), imap)
    tail = lambda imap: pl.BlockSpec((None, span, DIL_HD), imap)
    u3 = u.reshape(b, s, U_B)
    kv3 = kv.reshape(b, s, KV_W)
    o, lse = pl.pallas_call(
        functools.partial(_dil_prompt_body, gi=gi, dil=dil),
        grid=(b, s // DIL_TB, DIL_QH),
        in_specs=[pl.BlockSpec(memory_space=pltpu.SMEM),
                  blk(lambda i, j, h: (i, j, q0 + h)),
                  tail(lambda i, j, h: (i, jnp.maximum(j * per_tb - 1, 0), k0 + h // qpk)),
                  tail(lambda i, j, h: (i, jnp.maximum(j * per_tb - 1, 0), v0 + h // qpk)),
                  blk(lambda i, j, h: (i, j, k0 + h // qpk)),
                  blk(lambda i, j, h: (i, j, v0 + h // qpk))],
        out_specs=[blk(lambda i, j, h: (i, j, h)),
                   pl.BlockSpec((None, DIL_TB, LANES), lambda i, j, h: (i, j, 0))],
        out_shape=[jax.ShapeDtypeStruct((b, s, DIL_GW), F32), jax.ShapeDtypeStruct((b, s, LANES), F32)],
        compiler_params=_params("parallel", "parallel", "arbitrary"),
        name=f"dil_attn_prompt_g{gi}",
    )(slopes, u3, kv3, kv3, kv3, kv3)
    return o.reshape(b * s, DIL_GW), lse.reshape(b * s, LANES)


DIL_BT = 8
DIL_TOGETHER = 4


def _dil_sample_body(sl_ref, q_ref, buf_ref, new_ref, o_ref, l_ref, knew_ref, vnew_ref, *, gi, dil, tlen):
    rows = tlen * DIL_QH
    qpk = DIL_QH // DIL_KVH
    if len(buf_ref.shape) == 4:
        per_pos = buf_ref.shape[2]
        buf_ref = buf_ref.reshape(DIL_BT, DIL_NK * per_pos, DIL_HD)
    else:
        per_pos = buf_ref.shape[1] // DIL_NK
    ri = lax.broadcasted_iota(jnp.int32, (rows, 1), 0)
    tok, head = ri // DIL_QH, ri % DIL_QH
    kvh_of_row = head // qpk
    lane = lax.broadcasted_iota(jnp.int32, (rows, DIL_NK), 1)
    neg_slope = jnp.zeros((rows, 1), F32)
    for h in range(DIL_QH):
        neg_slope = jnp.where(head == h, -sl_ref[gi * DIL_QH + h], neg_slope)
    if dil == 1:
        j_buf, ok_buf = DIL_NK + tok - lane, lane >= tok
        j_new, ok_new = tok - lane, lane <= tok
    else:
        j_buf, ok_buf = DIL_NK - lane, None
        j_new, ok_new = jnp.zeros_like(lane), lane == tok
    bias_buf = neg_slope * (dil * j_buf).astype(F32)
    bias_new = neg_slope * (dil * j_new).astype(F32)
    scale = DIL_HD ** -0.5
    if dil == 1:
        tiles = [(0, kvh, kvh_of_row == kvh) for kvh in range(DIL_KVH)]
    else:
        tiles = [(t, kvh, (tok == t) & (kvh_of_row == kvh)) for t in range(tlen) for kvh in range(DIL_KVH)]
    first_kvh = kvh_of_row == 0
    knew_ref[...] = jnp.zeros_like(knew_ref)
    vnew_ref[...] = jnp.zeros_like(vnew_ref)

    def buf_tile(bb, res, kv, kvh):
        return buf_ref[bb, pl.ds((res * 2 + kv) * DIL_KVH + kvh, DIL_NK, stride=per_pos), :].astype(BF16)

    def scores(bb, slot):
        q = q_ref[bb].astype(BF16)
        new = new_ref[bb]
        for kvh in range(DIL_KVH):
            knew_ref[slot, kvh, 0:tlen, :] = new[:, kvh * DIL_HD:(kvh + 1) * DIL_HD]
            vnew_ref[slot, kvh, 0:tlen, :] = new[:, (DIL_KVH + kvh) * DIL_HD:(DIL_KVH + kvh + 1) * DIL_HD]
        s_buf = jnp.zeros((rows, DIL_NK), F32)
        for res, kvh, mine in tiles:
            s_buf = jnp.where(mine, _dot_nt(q, buf_tile(bb, res, 0, kvh)), s_buf)
        s_new = jnp.where(first_kvh, _dot_nt(q, knew_ref[slot, 0].astype(BF16)),
                          _dot_nt(q, knew_ref[slot, 1].astype(BF16)))
        return s_buf, s_new

    def weights(s_buf, s_new):
        s_buf = s_buf * scale + bias_buf
        if ok_buf is not None:
            s_buf = jnp.where(ok_buf, s_buf, -jnp.inf)
        s_new = jnp.where(ok_new, s_new * scale + bias_new, -jnp.inf)
        mx = jnp.maximum(jnp.max(s_buf, axis=-1, keepdims=True), jnp.max(s_new, axis=-1, keepdims=True))
        e_buf = jnp.exp(s_buf - mx)
        e_new = jnp.exp(s_new - mx)
        den = jnp.sum(e_buf, axis=-1, keepdims=True) + jnp.sum(e_new, axis=-1, keepdims=True)
        return (e_buf / den).astype(BF16), (e_new / den).astype(BF16), mx + jnp.log(den)

    def finish(bb, slot, p_buf, p_new, lse):
        o = jnp.where(first_kvh, _dot(p_new, vnew_ref[slot, 0].astype(BF16)),
                      _dot(p_new, vnew_ref[slot, 1].astype(BF16)))
        for res, kvh, mine in tiles:
            o = o + jnp.where(mine, _dot(p_buf, buf_tile(bb, res, 1, kvh)), 0.0)
        o_ref[bb] = o
        l_ref[bb] = jnp.broadcast_to(lse, (rows, DIL_HD))

    def per_group(g, carry):
        members = [(g * DIL_TOGETHER + k, k) for k in range(DIL_TOGETHER)]
        scored = [scores(bb, slot) for bb, slot in members]
        weighted = [weights(*sc) for sc in scored]
        for (bb, slot), w in zip(members, weighted):
            finish(bb, slot, *w)
        return carry

    lax.fori_loop(0, DIL_BT // DIL_TOGETHER, per_group, 0)


def _dil_attn_sample(slopes, u, kv_new, win_buf, gi, b, tlen):
    window, dil = DIL_CONFIGS[gi]
    lb = win_buf.shape[1]
    assert window // dil == DIL_NK and lb == window and (dil == 1 or tlen <= dil) and DIL_KVH == 2
    rows = tlen * DIL_QH
    per_res = 2 * DIL_KVH
    fetch = 1 if dil == 1 else tlen
    if fetch == dil:
        buf = win_buf.reshape(b, DIL_NK * dil * per_res, DIL_HD)
        buf_spec = pl.BlockSpec((DIL_BT, DIL_NK * dil * per_res, DIL_HD), lambda i: (i, 0, 0))
    else:
        assert (fetch * per_res) % 8 == 0
        buf = win_buf.reshape(b, DIL_NK, dil * per_res, DIL_HD)
        buf_spec = pl.BlockSpec((DIL_BT, DIL_NK, fetch * per_res, DIL_HD), lambda i: (i, 0, 0, 0))
    q0 = MEM_Q + gi * DIL_GW
    q = u[:, q0:q0 + DIL_GW].reshape(b, rows, DIL_HD)
    row_blk = pl.BlockSpec((DIL_BT, rows, DIL_HD), lambda i: (i, 0, 0))
    o, lse = pl.pallas_call(
        functools.partial(_dil_sample_body, gi=gi, dil=dil, tlen=tlen),
        grid=(b // DIL_BT,),
        in_specs=[pl.BlockSpec(memory_space=pltpu.SMEM), row_blk, buf_spec,
                  pl.BlockSpec((DIL_BT, tlen, DIL_GW), lambda i: (i, 0, gi))],
        out_specs=[row_blk, row_blk],
        out_shape=[jax.ShapeDtypeStruct((b, rows, DIL_HD), F32)] * 2,
        scratch_shapes=[pltpu.VMEM((DIL_TOGETHER, DIL_KVH, DIL_NK, DIL_HD), F32)] * 2,
        compiler_params=_params("parallel"),
        name=f"dil_attn_sample_g{gi}",
    )(slopes, q, buf, kv_new.reshape(b, tlen, KV_W))
    lse = jnp.pad(lse[:, :, 0].reshape(b * tlen, DIL_QH), ((0, 0), (0, LANES - DIL_QH)))
    return o.reshape(b * tlen, DIL_GW), lse


def _pair_cols(a, pair, lane):
    h0 = 2 * pair
    return jnp.where(lane < SSD_HEADDIM, a[:, h0:h0 + 1], a[:, h0 + 1:h0 + 2])


X_TILES = D_INNER // LANES
B_TILE0 = X_TILES
C_TILE0 = X_TILES + BC_DIM // LANES
XBC_TILES = CONV_DIM // LANES


def _ssd_intra(xbc_ref, dt_raw, par_ref, tri_ref, seg_ref, y_ref, xddt_ref):
    dt_bias, a_log = par_ref[0:1, 0:LANES], par_ref[1:2, 0:LANES]
    pre = dt_raw + dt_bias
    dt = jnp.maximum(pre, 0.0) + jnp.log(1.0 + jnp.exp(-jnp.abs(pre)))
    d_a = dt * -jnp.exp(a_log)
    tri = tri_ref[...]
    acum = jnp.dot(tri, d_a, precision=HIGHEST, preferred_element_type=F32)
    a_end = jnp.dot(seg_ref[...], d_a, precision=HIGHEST, preferred_element_type=F32)
    acum_t = acum.T
    causal = tri > 0.5
    e_acum = jnp.exp(acum)
    e_end = jnp.exp(a_end)
    d_end = jnp.exp(a_end - acum)
    lane = lax.broadcasted_iota(jnp.int32, (CHUNK, LANES), 1)
    for grp in range(SSD_GROUPS):
        cb = _dot_nt(xbc_ref[C_TILE0 + grp].astype(BF16), xbc_ref[B_TILE0 + grp].astype(BF16))
        for half in range(2):
            pair = 2 * grp + half
            plo = pair * LANES
            xs = xbc_ref[pair]
            xd = xs * _pair_cols(dt, pair, lane)
            xd16 = xd.astype(BF16)
            yd = []
            for k in range(2):
                h = 2 * pair + k
                decay = jnp.exp(jnp.where(causal, acum[:, h:h + 1] - acum_t[h:h + 1, :], -jnp.inf))
                yd.append(_dot((cb * decay).astype(BF16), xd16))
            y_ref[:, plo:plo + LANES] = jnp.where(lane < SSD_HEADDIM, yd[0], yd[1]) + par_ref[2:3, plo:plo + LANES] * xs
            xddt_ref[plo:plo + LANES, :] = (xd * _pair_cols(d_end, pair, lane)).T
    return e_acum, e_end


def _ssd_finish(y_ref, z_ref, par_ref, o_ref):
    o_ref[...] = _rms(y_ref[...] * _silu(z_ref[...].astype(F32)), par_ref[3:4, :]).astype(o_ref.dtype)


def _ssd_prompt_body(z_ref, x_ref, b_ref, c_ref, dt_ref, cw_ref, par_ref, tri_ref, seg_ref,
                     o_ref, st_ref, ext_ref, xbc_ref, y_ref, xddt_ref):
    first = pl.program_id(1) == 0

    @pl.when(first)
    def _():
        ext_ref[:, 0:8, :] = jnp.zeros((XBC_TILES, 8, LANES), F32)
        st_ref[...] = jnp.zeros_like(st_ref)

    for t in range(XBC_TILES):
        lo = t * LANES
        if t < B_TILE0:
            ext_ref[t, 8:8 + CHUNK, :] = x_ref[:, lo:lo + LANES]
        elif t < C_TILE0:
            ext_ref[t, 8:8 + CHUNK, :] = b_ref[:, lo - D_INNER:lo - D_INNER + LANES]
        else:
            ext_ref[t, 8:8 + CHUNK, :] = c_ref[:, lo - D_INNER - BC_DIM:lo - D_INNER - BC_DIM + LANES]

    for t in range(XBC_TILES):
        acc = cw_ref[t, CONV_W:CONV_W + 1, :] + ext_ref[t, 5:5 + CHUNK, :] * cw_ref[t, 0:1, :]
        for k in range(1, CONV_W):
            acc = acc + ext_ref[t, 5 + k:5 + k + CHUNK, :] * cw_ref[t, k:k + 1, :]
        xbc_ref[t] = _silu(acc)
    ext_ref[:, 0:8, :] = ext_ref[:, CHUNK:CHUNK + 8, :]

    e_acum, e_end = _ssd_intra(xbc_ref, dt_ref[...], par_ref, tri_ref, seg_ref, y_ref, xddt_ref)

    lane = lax.broadcasted_iota(jnp.int32, (CHUNK, LANES), 1)
    gw = SSD_STATE * 2
    for grp in range(SSD_GROUPS):
        rows = slice(grp * gw, (grp + 1) * gw)
        prev = st_ref[rows, :]
        y_off = _dot_nt(xbc_ref[C_TILE0 + grp].astype(BF16), prev.astype(BF16))
        for half in range(2):
            pair = 2 * grp + half
            plo = pair * LANES
            y_ref[:, plo:plo + LANES] += y_off[:, half * LANES:(half + 1) * LANES] * _pair_cols(e_acum, pair, lane)
        new = _dot(xddt_ref[rows, :].astype(BF16), xbc_ref[B_TILE0 + grp].astype(BF16))
        for k in range(4):
            h = 4 * grp + k
            hr = slice(grp * gw + k * SSD_HEADDIM, grp * gw + (k + 1) * SSD_HEADDIM)
            dec = jnp.broadcast_to(jnp.broadcast_to(e_end[:, h:h + 1], (CHUNK, SSD_STATE))[0:1, :],
                                   (SSD_HEADDIM, SSD_STATE))
            st_ref[hr, :] = prev[k * SSD_HEADDIM:(k + 1) * SSD_HEADDIM, :] * dec + new[k * SSD_HEADDIM:(k + 1) * SSD_HEADDIM, :]

    _ssd_finish(y_ref, z_ref, par_ref, o_ref)


def _ssd_consts(segment):
    r = jnp.arange(CHUNK)
    same = (r[:, None] // segment) == (r[None, :] // segment)
    return (same & (r[None, :] <= r[:, None])).astype(F32), same.astype(F32)


def _ssd_prompt(u_zq, u_xd, conv_wb, par, b, s):
    assert s % CHUNK == 0
    tri, seg = _ssd_consts(CHUNK)
    zq3 = u_zq.reshape(b, s, U_ZQ)
    xd3 = u_xd.reshape(b, s, U_XD)
    col = lambda width, blk: pl.BlockSpec((None, CHUNK, width), lambda i, c: (i, c, blk))
    y, st = pl.pallas_call(
        _ssd_prompt_body,
        grid=(b, s // CHUNK),
        in_specs=[col(D_INNER, 0), col(D_INNER, 0), col(BC_DIM, D_INNER // BC_DIM),
                  col(BC_DIM, D_INNER // BC_DIM + 1), col(LANES, CONV_DIM // LANES),
                  _const_spec((XBC_TILES, 8, LANES)), _const_spec((4, D_INNER)),
                  _const_spec((CHUNK, CHUNK)), _const_spec((CHUNK, CHUNK))],
        out_specs=[pl.BlockSpec((None, CHUNK, D_INNER), lambda i, c: (i, c, 0)),
                   pl.BlockSpec((None, D_INNER, SSD_STATE), lambda i, c: (i, 0, 0))],
        out_shape=[jax.ShapeDtypeStruct((b, s, D_INNER), BF16),
                   jax.ShapeDtypeStruct((b, D_INNER, SSD_STATE), F32)],
        scratch_shapes=[pltpu.VMEM((XBC_TILES, CHUNK + 8, LANES), F32), pltpu.VMEM((XBC_TILES, CHUNK, LANES), F32),
                        pltpu.VMEM((CHUNK, D_INNER), F32), pltpu.VMEM((D_INNER, CHUNK), F32)],
        compiler_params=_params("parallel", "arbitrary"),
        name="ssd_prompt",
    )(zq3, xd3, xd3, xd3, xd3, conv_wb, par, tri, seg)
    return y.reshape(b * s, D_INNER), st


def _conv_sample_body(seq_ref, cw_ref, o_ref, *, tlen):
    acc = cw_ref[4:5, :] + seq_ref[:, 0:tlen, :] * cw_ref[0:1, :]
    for k in range(1, CONV_W):
        acc = acc + seq_ref[:, k:k + tlen, :] * cw_ref[k:k + 1, :]
    o_ref[...] = _silu(acc)


def _conv_sample(seq, conv_wb, tlen):
    b = seq.shape[0]
    bt = 8
    return pl.pallas_call(
        functools.partial(_conv_sample_body, tlen=tlen),
        grid=(b // bt,),
        in_specs=[pl.BlockSpec((bt, CONV_W - 1 + tlen, CONV_DIM), lambda i: (i, 0, 0)),
                  _const_spec((8, CONV_DIM))],
        out_specs=pl.BlockSpec((bt, tlen, CONV_DIM), lambda i: (i, 0, 0)),
        out_shape=jax.ShapeDtypeStruct((b, tlen, CONV_DIM), F32),
        compiler_params=_params("parallel"),
        name="conv_sample",
    )(seq, conv_wb)


SSD_BT = 8


def _ssd_sample_body(z_ref, xbc_in_ref, dt_ref, par_ref, tri_ref, seg_ref, st_in_ref,
                     o_ref, st_out_ref, xbc_ref, dtp_ref, zp_ref, y_ref, xddt_ref, op_ref, *, tlen):
    rows = SSD_BT * tlen
    xbc_ref[...] = jnp.zeros_like(xbc_ref)
    dtp_ref[...] = jnp.zeros_like(dtp_ref)
    zp_ref[...] = jnp.zeros_like(zp_ref)
    for t in range(XBC_TILES):
        xbc_ref[t, 0:rows, :] = xbc_in_ref[:, t * LANES:(t + 1) * LANES]
    dtp_ref[0:rows, :] = dt_ref[...]
    zp_ref[0:rows, :] = z_ref[...].astype(F32)
    e_acum, e_end = _ssd_intra(xbc_ref, dtp_ref[...], par_ref, tri_ref, seg_ref, y_ref, xddt_ref)

    lane = lax.broadcasted_iota(jnp.int32, (CHUNK, LANES), 1)
    row_b = lax.broadcasted_iota(jnp.int32, (CHUNK, 1), 0) // tlen
    win = 16
    per_win = win // tlen
    win_b = lax.broadcasted_iota(jnp.int32, (win, 1), 0) // tlen
    gw = SSD_STATE * 2
    e_end_b = [jnp.broadcast_to(e_end[:, h:h + 1], (CHUNK, SSD_STATE)) for h in range(SSD_HEADS)]
    for grp in range(SSD_GROUPS):
        rsl = slice(grp * gw, (grp + 1) * gw)
        c16 = xbc_ref[C_TILE0 + grp].astype(BF16)
        b32 = xbc_ref[B_TILE0 + grp]
        xddt16 = xddt_ref[rsl, :].astype(BF16)
        for w in range(rows // win):
            c_win = c16[w * win:(w + 1) * win, :]
            y_off = jnp.zeros((win, gw), F32)
            for k in range(per_win):
                bb = w * per_win + k
                prev = st_in_ref[bb, rsl, :]
                y_off = jnp.where(win_b == k, _dot_nt(c_win, prev.astype(BF16)), y_off)
                new = _dot(xddt16, jnp.where(row_b == bb, b32, 0.0).astype(BF16))
                for hh in range(4):
                    h = 4 * grp + hh
                    hs = slice(hh * SSD_HEADDIM, (hh + 1) * SSD_HEADDIM)
                    dec = jnp.broadcast_to(e_end_b[h][bb * tlen:bb * tlen + 1, :], (SSD_HEADDIM, SSD_STATE))
                    st_out_ref[bb, grp * gw + hh * SSD_HEADDIM:grp * gw + (hh + 1) * SSD_HEADDIM, :] = (
                        prev[hs, :] * dec + new[hs, :])
            for half in range(2):
                pair = 2 * grp + half
                plo = pair * LANES
                scale = _pair_cols(e_acum, pair, lane)[w * win:(w + 1) * win, :]
                y_ref[w * win:(w + 1) * win, plo:plo + LANES] += y_off[:, half * LANES:(half + 1) * LANES] * scale

    _ssd_finish(y_ref, zp_ref, par_ref, op_ref)
    o_ref[...] = op_ref[0:rows, :].astype(o_ref.dtype)


def _ssd_sample(u_zq, xbc, u_xd, par, state, layer, b, tlen):
    rows = SSD_BT * tlen
    assert CHUNK % rows == 0 and 16 % tlen == 0 and rows % 16 == 0
    tri, seg = _ssd_consts(tlen)
    row = lambda width, blk: pl.BlockSpec((rows, width), lambda i: (i, blk))
    st_spec = pl.BlockSpec((SSD_BT, D_INNER, SSD_STATE), lambda i: (i, 0, 0))
    st_in_spec = pl.BlockSpec((None, SSD_BT, D_INNER, SSD_STATE), lambda i: (layer, i, 0, 0))
    y, st = pl.pallas_call(
        functools.partial(_ssd_sample_body, tlen=tlen),
        grid=(b // SSD_BT,),
        in_specs=[row(D_INNER, 0), row(CONV_DIM, 0), row(LANES, CONV_DIM // LANES), _const_spec((4, D_INNER)),
                  _const_spec((CHUNK, CHUNK)), _const_spec((CHUNK, CHUNK)), st_in_spec],
        out_specs=[row(D_INNER, 0), st_spec],
        out_shape=[jax.ShapeDtypeStruct((b * tlen, D_INNER), BF16),
                   jax.ShapeDtypeStruct((b, D_INNER, SSD_STATE), F32)],
        scratch_shapes=[pltpu.VMEM((XBC_TILES, CHUNK, LANES), F32), pltpu.VMEM((CHUNK, LANES), F32),
                        pltpu.VMEM((CHUNK, D_INNER), F32), pltpu.VMEM((CHUNK, D_INNER), F32),
                        pltpu.VMEM((D_INNER, CHUNK), F32), pltpu.VMEM((CHUNK, D_INNER), F32)],
        compiler_params=_params("parallel"),
        name="ssd_sample",
    )(u_zq, xbc, u_xd, par, tri, seg, state)
    return y, st


def _prep_weights(norm_g, w_ffn_gu, w_ffn_down, w_in_a, conv_w, conv_b, dt_bias, a_log, d_skip, ssd_norm_g,
                  w_out_a, w_in_b, w_out_b, kv_norm_g, w_kv_shared, mem_norm_g, w_mem_kv, final_norm_g):
    p = {}
    p["g_ffn"] = [[jnp.stack([norm_g[l, 0 + 4 * k], norm_g[l, 1 + 4 * k], final_norm_g]) for k in range(2)]
                  for l in range(DEPTH)]
    p["w_gu"] = w_ffn_gu.astype(BF16)
    p["w_down"] = w_ffn_down.astype(BF16)
    p["w_zq"] = jnp.concatenate([w_in_a[:, :, :D_INNER], w_in_a[:, :, SSD_IN:]], axis=-1).astype(BF16)
    w_dt = jnp.pad(w_in_a[:, :, SSD_MAIN:SSD_IN], ((0, 0), (0, 0), (0, LANES - SSD_HEADS)))
    p["w_xd"] = jnp.concatenate([w_in_a[:, :, D_INNER:SSD_MAIN], w_dt], axis=-1).astype(BF16)
    p["g_in_a"] = norm_g[:N_A, 2]
    p["g_in_b"] = norm_g[N_A:, 2]
    p["conv_wb"] = jnp.concatenate([conv_w, conv_b[:, None, :], jnp.zeros((N_A, 3, CONV_DIM), F32)], axis=1)
    p["conv_wb_tiles"] = p["conv_wb"].reshape(N_A, 8, XBC_TILES, LANES).transpose(0, 2, 1, 3)
    lane_pad = lambda v: jnp.pad(v, ((0, 0), (0, D_INNER - v.shape[1])))
    p["ssd_par"] = jnp.stack([lane_pad(dt_bias), lane_pad(a_log), jnp.repeat(d_skip, SSD_HEADDIM, axis=1),
                              ssd_norm_g], axis=1)
    p["w_out_a"] = w_out_a.astype(BF16)
    q_dil = N_DIL * DIL_GW
    p["w_in_b"] = jnp.concatenate([w_in_b[:, :, q_dil:], w_in_b[:, :, :q_dil]], axis=-1).astype(BF16)
    p["w_out_b"] = w_out_b.astype(BF16)
    p["w_kv"] = w_kv_shared.astype(BF16)
    p["w_mem_kv"] = w_mem_kv.astype(BF16)
    n = N_DIL * DIL_QH
    p["slopes"] = jnp.exp2(-8.0 * jnp.arange(1, n + 1, dtype=F32) / n)
    p["norm_g"] = norm_g
    p["kv_norm_g"] = kv_norm_g
    p["mem_norm_g"] = mem_norm_g
    return p


def _trunk(p, x, b, s, mem_kv, conv_prev, ssm_prev, win_bufs):
    prompt = win_bufs is None
    ssm_new, conv_new = [], []
    kv = None
    for l in range(DEPTH):
        g = p["norm_g"][l]
        x = _ffn(x, p["g_ffn"][l][0], p["w_gu"], p["w_down"], l, 0)
        if l < N_A:
            u_zq = _norm_matmul(x, p["g_in_a"], p["w_zq"], layers=(l,), out_dtype=BF16)[0]
            u_xd = _norm_matmul(x, p["g_in_a"], p["w_xd"], layers=(l,))[0]
            raw = u_xd.reshape(b, s, U_XD)
            if prompt:
                y_mix, st = _ssd_prompt(u_zq, u_xd, p["conv_wb_tiles"][l], p["ssd_par"][l], b, s)
                conv_new.append(raw[:, s - (CONV_W - 1):, :CONV_DIM])
                mem_o = _mem_attn_prompt(u_zq, D_INNER // MEM_Q, mem_kv, l, b, s)
            else:
                seq = jnp.concatenate([conv_prev[l], raw[:, :, :CONV_DIM]], axis=1)
                xbc = _conv_sample(seq, p["conv_wb"][l], s).reshape(b * s, CONV_DIM)
                y_mix, st = _ssd_sample(u_zq, xbc, u_xd, p["ssd_par"][l], ssm_prev, l, b, s)
                conv_new.append(seq[:, s:])
                mem_o = _mem_attn_sample(u_zq, D_INNER // MEM_Q, mem_kv, l, b, s)
            ssm_new.append(st.reshape(b, SSD_HEADS, SSD_HEADDIM, SSD_STATE))
            x = _out_proj_a(y_mix, mem_o, x, g[3], p["w_out_a"], l)
        else:
            i = l - N_A
            u = _norm_matmul(x, p["g_in_b"], p["w_in_b"], layers=(i,))[0]
            outs, lses = [], []
            for gi in range(N_DIL):
                if prompt:
                    o, lse = _dil_attn_prompt(p["slopes"], u, kv, gi, b, s)
                else:
                    o, lse = _dil_attn_sample(p["slopes"], u, kv, win_bufs[gi], gi, b, s)
                outs.append(o)
                lses.append(lse)
            if prompt:
                mem_o = _mem_attn_prompt(u, 0, mem_kv, l, b, s)
            else:
                mem_o = _mem_attn_sample(u, 0, mem_kv, l, b, s)
            x = _out_proj_b(outs, lses, mem_o, x, g[3], p["w_out_b"], i)
        x = _ffn(x, p["g_ffn"][l][1], p["w_gu"], p["w_down"], l, 1, final=(l == DEPTH - 1))
        if l == N_A - 1:
            kv = _norm_matmul(x, p["kv_norm_g"][None], p["w_kv"][None])[0]
    return x, jnp.stack(conv_new), jnp.stack(ssm_new), kv


def kernel(x_prompt, x_sample, mem_prompt, cache_mem_kv, state_ssm, state_conv, cache_win_g1, cache_win_g2,
           cache_win_g3, norm_g, w_ffn_gu, w_ffn_down, w_in_a, conv_w, conv_b, dt_bias, a_log, d_skip,
           ssd_norm_g, w_out_a, w_in_b, w_out_b, kv_norm_g, w_kv_shared, mem_norm_g, w_mem_kv, final_norm_g):
    p = _prep_weights(norm_g, w_ffn_gu, w_ffn_down, w_in_a, conv_w, conv_b, dt_bias, a_log, d_skip, ssd_norm_g,
                      w_out_a, w_in_b, w_out_b, kv_norm_g, w_kv_shared, mem_norm_g, w_mem_kv, final_norm_g)
    bp, sp = x_prompt.shape[:2]
    bs, ss = x_sample.shape[:2]

    mem_flat = mem_prompt.reshape(bp * N_MEM, D_MODEL)
    mem_kv_p = _norm_matmul(mem_flat, mem_norm_g, p["w_mem_kv"]).reshape(DEPTH, bp, N_MEM, 2 * MEM_Q)
    y_p, conv_p, ssm_p, kv_p = _trunk(p, x_prompt.reshape(bp * sp, D_MODEL), bp, sp, mem_kv_p, None, None, None)
    kv_p = kv_p.reshape(bp, sp, KV_W)
    win_p = [kv_p[:, sp - min(w, sp):, gi * DIL_GW:(gi + 1) * DIL_GW].reshape(bp, min(w, sp), 2, DIL_KVH, DIL_HD)
             for gi, (w, _) in enumerate(DIL_CONFIGS)]

    y_s, conv_s, ssm_s, kv_s = _trunk(p, x_sample.reshape(bs * ss, D_MODEL), bs, ss, cache_mem_kv, state_conv,
                                      state_ssm.reshape(N_A, bs, D_INNER, SSD_STATE),
                                      (cache_win_g1, cache_win_g2, cache_win_g3))
    kv_s = kv_s.reshape(bs, ss, N_DIL, 2, DIL_KVH, DIL_HD)
    return (y_p.reshape(bp, sp, D_MODEL), y_s.reshape(bs, ss, D_MODEL),
            mem_kv_p.reshape(DEPTH, bp, N_MEM, 2, MEM_HEADS, MEM_HD), ssm_p, conv_p, win_p[0], win_p[1], win_p[2],
            ssm_s, conv_s, kv_s[:, :, 0], kv_s[:, :, 1], kv_s[:, :, 2])
```

```python
import functools

import jax
import jax.numpy as jnp
from jax import lax
from jax.experimental import pallas as pl
from jax.experimental.pallas import tpu as pltpu

F32 = jnp.float32
BF16 = jnp.bfloat16
HIGHEST = lax.Precision.HIGHEST

D_MODEL = 1024
DEPTH = 4
N_A = 2
D_FF = 2816
D_INNER = 2048
SSD_HEADDIM = 64
SSD_HEADS = 32
SSD_GROUPS = 8
SSD_STATE = 128
CONV_W = 4
CONV_DIM = D_INNER + 2 * SSD_GROUPS * SSD_STATE
BC_DIM = SSD_GROUPS * SSD_STATE
DIL_CONFIGS = ((128, 1), (512, 4), (2048, 16))
N_DIL = 3
DIL_QH = 4
DIL_KVH = 2
DIL_HD = 128
DIL_NK = 128
DIL_GW = DIL_QH * DIL_HD
N_MEM = 256
MEM_HEADS = 4
MEM_HD = 256
MEM_Q = MEM_HEADS * MEM_HD
EPS = 1e-6
SSD_MAIN = 2 * D_INNER + 2 * BC_DIM
SSD_IN = SSD_MAIN + SSD_HEADS
U_ZQ = D_INNER + MEM_Q
U_XD = CONV_DIM + 128
U_B = MEM_Q + N_DIL * DIL_GW
KV_W = N_DIL * 2 * DIL_KVH * DIL_HD

LANES = 128
CHUNK = 128
FF_CHUNK = 256
VMEM_LIMIT = 56 * 2 ** 20


def _params(*sem):
    return pltpu.CompilerParams(dimension_semantics=sem, vmem_limit_bytes=VMEM_LIMIT)


def _const_spec(shape):
    return pl.BlockSpec(shape, lambda *_: (0,) * len(shape), pipeline_mode=pl.Buffered(1))


def _rms(x, g):
    return x * lax.rsqrt(jnp.mean(x * x, axis=-1, keepdims=True) + EPS) * g


def _silu(x):
    return x / (1.0 + jnp.exp(-x))


def _dot(a, b):
    return jnp.dot(a, b, preferred_element_type=F32)


def _dot_nt(a, b):
    return lax.dot_general(a, b, (((1,), (1,)), ((), ())), preferred_element_type=F32)


def _row_tile(t, pref):
    return pref if t % pref == 0 else t


def _ffn_body(x_ref, g_ref, wgu_ref, wd_ref, o_ref, acc_ref, *, final):
    x = x_ref[...]
    h = _rms(x, g_ref[0:1, :]).astype(BF16)
    for c in range(D_FF // FF_CHUNK):
        lo = c * FF_CHUNK
        gate = _dot(h, wgu_ref[:, lo:lo + FF_CHUNK])
        up = _dot(h, wgu_ref[:, D_FF + lo:D_FF + lo + FF_CHUNK])
        part = _dot((_silu(gate) * up).astype(BF16), wd_ref[lo:lo + FF_CHUNK, :])
        if c == 0:
            acc_ref[...] = part
        else:
            acc_ref[...] += part
    y = x + 0.5 * _rms(acc_ref[...], g_ref[1:2, :])
    if final:
        y = _rms(y, g_ref[2:3, :])
    o_ref[...] = y


def _pick_spec(shape, *lead):
    return pl.BlockSpec((None,) * len(lead) + tuple(shape), lambda *_: tuple(lead) + (0,) * len(shape),
                        pipeline_mode=pl.Buffered(1))


def _ffn(x, g3, wgu, wd, layer, which, final=False):
    t = x.shape[0]
    tm = _row_tile(t, 512)
    return pl.pallas_call(
        functools.partial(_ffn_body, final=final),
        grid=(t // tm,),
        in_specs=[pl.BlockSpec((tm, D_MODEL), lambda i: (i, 0)),
                  _const_spec((3, D_MODEL)),
                  _pick_spec((D_MODEL, 2 * D_FF), layer, which),
                  _pick_spec((D_FF, D_MODEL), layer, which)],
        out_specs=pl.BlockSpec((tm, D_MODEL), lambda i: (i, 0)),
        out_shape=jax.ShapeDtypeStruct((t, D_MODEL), F32),
        scratch_shapes=[pltpu.VMEM((tm, D_MODEL), F32)],
        compiler_params=_params("parallel"),
        name="ffn",
    )(x, g3, wgu, wd)


def _nmm_body(x_ref, g_ref, w_ref, o_ref, h_ref):
    @pl.when(pl.program_id(2) == 0)
    def _():
        h_ref[...] = _rms(x_ref[...], g_ref[...]).astype(BF16)

    o_ref[...] = _dot(h_ref[...], w_ref[...]).astype(o_ref.dtype)


NMM_WHOLE_N = 4608


def _nmm_tiles(t, n):
    if n <= NMM_WHOLE_N:
        return _row_tile(t, 512 if n > 2560 else 1024), n
    return _row_tile(t, 1024), max(c for c in range(LANES, 1536 + 1, LANES) if n % c == 0)


def _norm_matmul(x, g, w, layers=None, out_dtype=F32):
    t, n = x.shape[0], w.shape[2]
    layers = tuple(range(w.shape[0])) if layers is None else tuple(layers)
    first, count = layers[0], len(layers)
    assert layers == tuple(range(first, first + count))
    tm, tn = _nmm_tiles(t, n)
    w_mode = dict(pipeline_mode=pl.Buffered(1)) if (tn == n and count == 1) else {}
    return pl.pallas_call(
        _nmm_body,
        grid=(count, t // tm, n // tn),
        in_specs=[pl.BlockSpec((tm, D_MODEL), lambda l, i, j: (i, 0)),
                  pl.BlockSpec((None, 1, D_MODEL), lambda l, i, j: (first + l, 0, 0)),
                  pl.BlockSpec((None, D_MODEL, tn), lambda l, i, j: (first + l, 0, j), **w_mode)],
        out_specs=pl.BlockSpec((None, tm, tn), lambda l, i, j: (l, i, j)),
        out_shape=jax.ShapeDtypeStruct((count, t, n), out_dtype),
        scratch_shapes=[pltpu.VMEM((tm, D_MODEL), BF16)],
        compiler_params=_params("parallel", "parallel", "arbitrary"),
        name="norm_matmul",
    )(x, g.reshape(g.shape[0], 1, D_MODEL), w)


def _out_a_body(y_ref, m_ref, x_ref, g_ref, w_ref, o_ref):
    acc = _dot(y_ref[...], w_ref[0:D_INNER, :])
    acc += _dot(m_ref[...], w_ref[D_INNER:D_INNER + MEM_Q, :])
    o_ref[...] = x_ref[...] + _rms(acc, g_ref[...])


def _out_proj_a(y, m, x, g, w, layer):
    t = x.shape[0]
    tm = _row_tile(t, 1024)
    row = lambda width: pl.BlockSpec((tm, width), lambda i: (i, 0))
    return pl.pallas_call(
        _out_a_body,
        grid=(t // tm,),
        in_specs=[row(D_INNER), row(MEM_Q), row(D_MODEL), _const_spec((1, D_MODEL)),
                  _pick_spec((D_INNER + MEM_Q, D_MODEL), layer)],
        out_specs=row(D_MODEL),
        out_shape=jax.ShapeDtypeStruct((t, D_MODEL), F32),
        compiler_params=_params("parallel"),
        name="out_proj_a",
    )(y, m, x, g.reshape(1, D_MODEL), w)


def _out_b_body(o1_ref, o2_ref, o3_ref, l1_ref, l2_ref, l3_ref, m_ref, x_ref, g_ref, w_ref, o_ref):
    l1, l2, l3 = l1_ref[...], l2_ref[...], l3_ref[...]
    mx = jnp.maximum(jnp.maximum(l1, l2), l3)
    e1, e2, e3 = jnp.exp(l1 - mx), jnp.exp(l2 - mx), jnp.exp(l3 - mx)
    den = e1 + e2 + e3
    a1, a2, a3 = e1 / den, e2 / den, e3 / den
    mix = []
    for h in range(DIL_QH):
        c = slice(h * DIL_HD, (h + 1) * DIL_HD)
        mix.append(a1[:, h:h + 1] * o1_ref[:, c] + a2[:, h:h + 1] * o2_ref[:, c] + a3[:, h:h + 1] * o3_ref[:, c])
    acc = _dot(jnp.concatenate(mix, axis=1).astype(BF16), w_ref[0:DIL_GW, :])
    acc += _dot(m_ref[...], w_ref[DIL_GW:DIL_GW + MEM_Q, :])
    o_ref[...] = x_ref[...] + _rms(acc, g_ref[...])


def _out_proj_b(outs, lses, m, x, g, w, layer):
    t = x.shape[0]
    tm = _row_tile(t, 1024)
    row = lambda width: pl.BlockSpec((tm, width), lambda i: (i, 0))
    return pl.pallas_call(
        _out_b_body,
        grid=(t // tm,),
        in_specs=[row(DIL_GW)] * 3 + [row(LANES)] * 3 + [row(MEM_Q), row(D_MODEL), _const_spec((1, D_MODEL)),
                                                         _pick_spec((DIL_GW + MEM_Q, D_MODEL), layer)],
        out_specs=row(D_MODEL),
        out_shape=jax.ShapeDtypeStruct((t, D_MODEL), F32),
        compiler_params=_params("parallel"),
        name="out_proj_b",
    )(*outs, *lses, m, x, g.reshape(1, D_MODEL), w)


def _softmax_rows(s):
    e = jnp.exp(s - jnp.max(s, axis=-1, keepdims=True))
    return e / jnp.sum(e, axis=-1, keepdims=True)


def _mem_prompt_body(q_ref, kv_ref, o_ref):
    cols = [slice(h * MEM_HD, (h + 1) * MEM_HD) for h in range(MEM_HEADS)]
    scores = [_dot_nt(q_ref[:, c].astype(BF16), kv_ref[:, c].astype(BF16)) * (MEM_HD ** -0.5) for c in cols]
    probs = [_softmax_rows(s).astype(BF16) for s in scores]
    for h, c in enumerate(cols):
        v = kv_ref[:, MEM_Q + h * MEM_HD:MEM_Q + (h + 1) * MEM_HD].astype(BF16)
        o_ref[:, c] = _dot(probs[h], v).astype(o_ref.dtype)


def _mem_attn_prompt(u, q_block, mem_kv, layer, b, s):
    c = u.shape[1]
    tq = _row_tile(s, 1024)
    out = pl.pallas_call(
        _mem_prompt_body,
        grid=(b, s // tq),
        in_specs=[pl.BlockSpec((None, tq, MEM_Q), lambda i, j: (i, j, q_block)),
                  pl.BlockSpec((None, None, N_MEM, 2 * MEM_Q), lambda i, j: (layer, i, 0, 0))],
        out_specs=pl.BlockSpec((None, tq, MEM_Q), lambda i, j: (i, j, 0)),
        out_shape=jax.ShapeDtypeStruct((b, s, MEM_Q), BF16),
        compiler_params=_params("parallel", "parallel"),
        name="mem_attn_prompt",
    )(u.reshape(b, s, c), mem_kv)
    return out.reshape(b * s, MEM_Q)


MEM_BT = 8
MEM_LT = MEM_HD // LANES
MEM_ROWS = 2 * MEM_LT * MEM_HEADS


def _mem_rows(kv_ref, bb, kv, h):
    parts = [kv_ref[bb, pl.ds((kv * MEM_LT + lt) * MEM_HEADS + h, N_MEM, stride=MEM_ROWS), :]
             for lt in range(MEM_LT)]
    return jnp.concatenate(parts, axis=1)


def _mem_sample_body(q_ref, kv_ref, o_ref, *, tlen):
    rows = MEM_BT * tlen
    owner = lax.broadcasted_iota(jnp.int32, (rows, 1), 0) // tlen
    q_all = q_ref[...]
    scores = []
    for h in range(MEM_HEADS):
        q = q_all[:, h * MEM_HD:(h + 1) * MEM_HD].astype(BF16)
        s = jnp.zeros((rows, N_MEM), F32)
        for bb in range(MEM_BT):
            k = _mem_rows(kv_ref, bb, 0, h).astype(BF16)
            s = jnp.where(owner == bb, _dot_nt(q, k), s)
        scores.append(s)
    probs = [_softmax_rows(s * (MEM_HD ** -0.5)).astype(BF16) for s in scores]
    for h in range(MEM_HEADS):
        o = jnp.zeros((rows, MEM_HD), F32)
        for bb in range(MEM_BT):
            v = _mem_rows(kv_ref, bb, 1, h).astype(BF16)
            o = jnp.where(owner == bb, _dot(probs[h], v), o)
        o_ref[:, h * MEM_HD:(h + 1) * MEM_HD] = o.astype(o_ref.dtype)


def _mem_attn_sample(u, q_block, mem_kv, layer, b, tlen):
    rows = MEM_BT * tlen
    depth = mem_kv.shape[0]
    mem_kv = mem_kv.reshape(depth, b, N_MEM, 2, MEM_HEADS, MEM_LT, LANES).transpose(0, 1, 2, 3, 5, 4, 6)
    mem_kv = mem_kv.reshape(depth, b, N_MEM * MEM_ROWS, LANES)
    return pl.pallas_call(
        functools.partial(_mem_sample_body, tlen=tlen),
        grid=(b // MEM_BT,),
        in_specs=[pl.BlockSpec((rows, MEM_Q), lambda i: (i, q_block)),
                  pl.BlockSpec((None, MEM_BT, N_MEM * MEM_ROWS, LANES), lambda i: (layer, i, 0, 0))],
        out_specs=pl.BlockSpec((rows, MEM_Q), lambda i: (i, 0)),
        out_shape=jax.ShapeDtypeStruct((b * tlen, MEM_Q), BF16),
        compiler_params=_params("parallel"),
        name="mem_attn_sample",
    )(u, mem_kv)


DIL_TB = 2048


def _dil_prompt_body(sl_ref, q_ref, kp_ref, vp_ref, kc_ref, vc_ref, o_ref, l_ref, *, gi, dil):
    span = DIL_NK * dil
    has_prev = pl.program_id(1) > 0
    head = pl.program_id(2)
    qi = lax.broadcasted_iota(jnp.int32, (DIL_NK, DIL_NK), 0)
    kj = lax.broadcasted_iota(jnp.int32, (DIL_NK, DIL_NK), 1)
    du_prev = DIL_NK + qi - kj
    du_own = qi - kj
    valid_prev = du_prev <= DIL_NK
    valid_own = du_own >= 0
    my_lane = kj == head
    neg_slope = -sl_ref[gi * DIL_QH + head]

    @pl.when(head == 0)
    def _():
        l_ref[...] = jnp.zeros_like(l_ref)

    bias_prev = neg_slope * (dil * du_prev).astype(F32)
    bias_own = neg_slope * (dil * du_own).astype(F32)
    scale = DIL_HD ** -0.5

    def rows(start):
        return pl.ds(start, DIL_NK) if dil == 1 else pl.ds(start, DIL_NK, stride=dil)

    def scores(blk, r):
        own = rows(blk * span + r)
        if blk == 0:
            k_prev, v_prev, ok_prev = kp_ref[rows(r), :], vp_ref[rows(r), :], valid_prev & has_prev
        else:
            prev = rows((blk - 1) * span + r)
            k_prev, v_prev, ok_prev = kc_ref[prev, :], vc_ref[prev, :], valid_prev
        q = q_ref[own, :].astype(BF16)
        s_prev = jnp.where(ok_prev, _dot_nt(q, k_prev.astype(BF16)) * scale + bias_prev, -jnp.inf)
        s_own = jnp.where(valid_own, _dot_nt(q, kc_ref[own, :].astype(BF16)) * scale + bias_own, -jnp.inf)
        mx = jnp.maximum(jnp.max(s_prev, axis=-1, keepdims=True), jnp.max(s_own, axis=-1, keepdims=True))
        return own, s_prev, s_own, mx, v_prev

    def weights(own, s_prev, s_own, mx, v_prev):
        e_prev = jnp.exp(s_prev - mx)
        e_own = jnp.exp(s_own - mx)
        den = jnp.sum(e_prev, axis=-1, keepdims=True) + jnp.sum(e_own, axis=-1, keepdims=True)
        return own, (e_prev / den).astype(BF16), (e_own / den).astype(BF16), mx + jnp.log(den), v_prev

    def finish(own, p_prev, p_own, lse, v_prev):
        o_ref[own, :] = _dot(p_prev, v_prev.astype(BF16)) + _dot(p_own, vc_ref[own, :].astype(BF16))
        l_ref[own, :] = jnp.where(my_lane, lse, l_ref[own, :])

    def streams(group):
        for st in [weights(*sc) for sc in [scores(blk, r) for blk, r in group]]:
            finish(*st)

    together = 8
    if dil <= together:
        todo = [(blk, r) for blk in range(DIL_TB // span) for r in range(dil)]
        for g0 in range(0, len(todo), together):
            streams(todo[g0:g0 + together])
    else:
        for blk in range(DIL_TB // span):
            def per_group(g, carry, blk=blk):
                streams([(blk, g * together + k) for k in range(together)])
                return carry

            lax.fori_loop(0, dil // together, per_group, 0)


def _dil_attn_prompt(slopes, u, kv, gi, b, s):
    window, dil = DIL_CONFIGS[gi]
    span = DIL_NK * dil
    assert window // dil == DIL_NK and s % DIL_TB == 0 and DIL_TB % span == 0
    per_tb = DIL_TB // span
    qpk = DIL_QH // DIL_KVH
    q0 = (MEM_Q + gi * DIL_GW) // DIL_HD
    k0 = gi * DIL_GW // DIL_HD
    v0 = k0 + DIL_KVH
    blk = lambda imap: pl.BlockSpec((None, DIL_TB, DIL_HD), imap)
    tail = lambda imap: pl.BlockSpec((None, span, DIL_HD), imap)
    u3 = u.reshape(b, s, U_B)
    kv3 = kv.reshape(b, s, KV_W)
    o, lse = pl.pallas_call(
        functools.partial(_dil_prompt_body, gi=gi, dil=dil),
        grid=(b, s // DIL_TB, DIL_QH),
        in_specs=[pl.BlockSpec(memory_space=pltpu.SMEM),
                  blk(lambda i, j, h: (i, j, q0 + h)),
                  tail(lambda i, j, h: (i, jnp.maximum(j * per_tb - 1, 0), k0 + h // qpk)),
                  tail(lambda i, j, h: (i, jnp.maximum(j * per_tb - 1, 0), v0 + h // qpk)),
                  blk(lambda i, j, h: (i, j, k0 + h // qpk)),
                  blk(lambda i, j, h: (i, j, v0 + h // qpk))],
        out_specs=[blk(lambda i, j, h: (i, j, h)),
                   pl.BlockSpec((None, DIL_TB, LANES), lambda i, j, h: (i, j, 0))],
        out_shape=[jax.ShapeDtypeStruct((b, s, DIL_GW), F32), jax.ShapeDtypeStruct((b, s, LANES), F32)],
        compiler_params=_params("parallel", "parallel", "arbitrary"),
        name=f"dil_attn_prompt_g{gi}",
    )(slopes, u3, kv3, kv3, kv3, kv3)
    return o.reshape(b * s, DIL_GW), lse.reshape(b * s, LANES)


DIL_BT = 8
DIL_TOGETHER = 4


def _dil_sample_body(sl_ref, q_ref, buf_ref, new_ref, o_ref, l_ref, knew_ref, vnew_ref, *, gi, dil, tlen):
    rows = tlen * DIL_QH
    qpk = DIL_QH // DIL_KVH
    if len(buf_ref.shape) == 4:
        per_pos = buf_ref.shape[2]
        buf_ref = buf_ref.reshape(DIL_BT, DIL_NK * per_pos, DIL_HD)
    else:
        per_pos = buf_ref.shape[1] // DIL_NK
    ri = lax.broadcasted_iota(jnp.int32, (rows, 1), 0)
    tok, head = ri // DIL_QH, ri % DIL_QH
    kvh_of_row = head // qpk
    lane = lax.broadcasted_iota(jnp.int32, (rows, DIL_NK), 1)
    neg_slope = jnp.zeros((rows, 1), F32)
    for h in range(DIL_QH):
        neg_slope = jnp.where(head == h, -sl_ref[gi * DIL_QH + h], neg_slope)
    if dil == 1:
        j_buf, ok_buf = DIL_NK + tok - lane, lane >= tok
        j_new, ok_new = tok - lane, lane <= tok
    else:
        j_buf, ok_buf = DIL_NK - lane, None
        j_new, ok_new = jnp.zeros_like(lane), lane == tok
    bias_buf = neg_slope * (dil * j_buf).astype(F32)
    bias_new = neg_slope * (dil * j_new).astype(F32)
    scale = DIL_HD ** -0.5
    if dil == 1:
        tiles = [(0, kvh, kvh_of_row == kvh) for kvh in range(DIL_KVH)]
    else:
        tiles = [(t, kvh, (tok == t) & (kvh_of_row == kvh)) for t in range(tlen) for kvh in range(DIL_KVH)]
    first_kvh = kvh_of_row == 0
    knew_ref[...] = jnp.zeros_like(knew_ref)
    vnew_ref[...] = jnp.zeros_like(vnew_ref)

    def buf_tile(bb, res, kv, kvh):
        return buf_ref[bb, pl.ds((res * 2 + kv) * DIL_KVH + kvh, DIL_NK, stride=per_pos), :].astype(BF16)

    def scores(bb, slot):
        q = q_ref[bb].astype(BF16)
        new = new_ref[bb]
        for kvh in range(DIL_KVH):
            knew_ref[slot, kvh, 0:tlen, :] = new[:, kvh * DIL_HD:(kvh + 1) * DIL_HD]
            vnew_ref[slot, kvh, 0:tlen, :] = new[:, (DIL_KVH + kvh) * DIL_HD:(DIL_KVH + kvh + 1) * DIL_HD]
        s_buf = jnp.zeros((rows, DIL_NK), F32)
        for res, kvh, mine in tiles:
            s_buf = jnp.where(mine, _dot_nt(q, buf_tile(bb, res, 0, kvh)), s_buf)
        s_new = jnp.where(first_kvh, _dot_nt(q, knew_ref[slot, 0].astype(BF16)),
                          _dot_nt(q, knew_ref[slot, 1].astype(BF16)))
        return s_buf, s_new

    def weights(s_buf, s_new):
        s_buf = s_buf * scale + bias_buf
        if ok_buf is not None:
            s_buf = jnp.where(ok_buf, s_buf, -jnp.inf)
        s_new = jnp.where(ok_new, s_new * scale + bias_new, -jnp.inf)
        mx = jnp.maximum(jnp.max(s_buf, axis=-1, keepdims=True), jnp.max(s_new, axis=-1, keepdims=True))
        e_buf = jnp.exp(s_buf - mx)
        e_new = jnp.exp(s_new - mx)
        den = jnp.sum(e_buf, axis=-1, keepdims=True) + jnp.sum(e_new, axis=-1, keepdims=True)
        return (e_buf / den).astype(BF16), (e_new / den).astype(BF16), mx + jnp.log(den)

    def finish(bb, slot, p_buf, p_new, lse):
        o = jnp.where(first_kvh, _dot(p_new, vnew_ref[slot, 0].astype(BF16)),
                      _dot(p_new, vnew_ref[slot, 1].astype(BF16)))
        for res, kvh, mine in tiles:
            o = o + jnp.where(mine, _dot(p_buf, buf_tile(bb, res, 1, kvh)), 0.0)
        o_ref[bb] = o
        l_ref[bb] = jnp.broadcast_to(lse, (rows, DIL_HD))

    def per_group(g, carry):
        members = [(g * DIL_TOGETHER + k, k) for k in range(DIL_TOGETHER)]
        scored = [scores(bb, slot) for bb, slot in members]
        weighted = [weights(*sc) for sc in scored]
        for (bb, slot), w in zip(members, weighted):
            finish(bb, slot, *w)
        return carry

    lax.fori_loop(0, DIL_BT // DIL_TOGETHER, per_group, 0)


def _dil_attn_sample(slopes, u, kv_new, win_buf, gi, b, tlen):
    window, dil = DIL_CONFIGS[gi]
    lb = win_buf.shape[1]
    assert window // dil == DIL_NK and lb == window and (dil == 1 or tlen <= dil) and DIL_KVH == 2
    rows = tlen * DIL_QH
    per_res = 2 * DIL_KVH
    fetch = 1 if dil == 1 else tlen
    if fetch == dil:
        buf = win_buf.reshape(b, DIL_NK * dil * per_res, DIL_HD)
        buf_spec = pl.BlockSpec((DIL_BT, DIL_NK * dil * per_res, DIL_HD), lambda i: (i, 0, 0))
    else:
        assert (fetch * per_res) % 8 == 0
        buf = win_buf.reshape(b, DIL_NK, dil * per_res, DIL_HD)
        buf_spec = pl.BlockSpec((DIL_BT, DIL_NK, fetch * per_res, DIL_HD), lambda i: (i, 0, 0, 0))
    q0 = MEM_Q + gi * DIL_GW
    q = u[:, q0:q0 + DIL_GW].reshape(b, rows, DIL_HD)
    row_blk = pl.BlockSpec((DIL_BT, rows, DIL_HD), lambda i: (i, 0, 0))
    o, lse = pl.pallas_call(
        functools.partial(_dil_sample_body, gi=gi, dil=dil, tlen=tlen),
        grid=(b // DIL_BT,),
        in_specs=[pl.BlockSpec(memory_space=pltpu.SMEM), row_blk, buf_spec,
                  pl.BlockSpec((DIL_BT, tlen, DIL_GW), lambda i: (i, 0, gi))],
        out_specs=[row_blk, row_blk],
        out_shape=[jax.ShapeDtypeStruct((b, rows, DIL_HD), F32)] * 2,
        scratch_shapes=[pltpu.VMEM((DIL_TOGETHER, DIL_KVH, DIL_NK, DIL_HD), F32)] * 2,
        compiler_params=_params("parallel"),
        name=f"dil_attn_sample_g{gi}",
    )(slopes, q, buf, kv_new.reshape(b, tlen, KV_W))
    lse = jnp.pad(lse[:, :, 0].reshape(b * tlen, DIL_QH), ((0, 0), (0, LANES - DIL_QH)))
    return o.reshape(b * tlen, DIL_GW), lse


def _pair_cols(a, pair, lane):
    h0 = 2 * pair
    return jnp.where(lane < SSD_HEADDIM, a[:, h0:h0 + 1], a[:, h0 + 1:h0 + 2])


X_TILES = D_INNER // LANES
B_TILE0 = X_TILES
C_TILE0 = X_TILES + BC_DIM // LANES
XBC_TILES = CONV_DIM // LANES


def _ssd_intra(xbc_ref, dt_raw, par_ref, tri_ref, seg_ref, y_ref, xddt_ref):
    dt_bias, a_log = par_ref[0:1, 0:LANES], par_ref[1:2, 0:LANES]
    pre = dt_raw + dt_bias
    dt = jnp.maximum(pre, 0.0) + jnp.log(1.0 + jnp.exp(-jnp.abs(pre)))
    d_a = dt * -jnp.exp(a_log)
    tri = tri_ref[...]
    acum = jnp.dot(tri, d_a, precision=HIGHEST, preferred_element_type=F32)
    a_end = jnp.dot(seg_ref[...], d_a, precision=HIGHEST, preferred_element_type=F32)
    acum_t = acum.T
    causal = tri > 0.5
    e_acum = jnp.exp(acum)
    e_end = jnp.exp(a_end)
    d_end = jnp.exp(a_end - acum)
    lane = lax.broadcasted_iota(jnp.int32, (CHUNK, LANES), 1)
    for grp in range(SSD_GROUPS):
        cb = _dot_nt(xbc_ref[C_TILE0 + grp].astype(BF16), xbc_ref[B_TILE0 + grp].astype(BF16))
        for half in range(2):
            pair = 2 * grp + half
            plo = pair * LANES
            xs = xbc_ref[pair]
            xd = xs * _pair_cols(dt, pair, lane)
            xd16 = xd.astype(BF16)
            yd = []
            for k in range(2):
                h = 2 * pair + k
                decay = jnp.exp(jnp.where(causal, acum[:, h:h + 1] - acum_t[h:h + 1, :], -jnp.inf))
                yd.append(_dot((cb * decay).astype(BF16), xd16))
            y_ref[:, plo:plo + LANES] = jnp.where(lane < SSD_HEADDIM, yd[0], yd[1]) + par_ref[2:3, plo:plo + LANES] * xs
            xddt_ref[plo:plo + LANES, :] = (xd * _pair_cols(d_end, pair, lane)).T
    return e_acum, e_end


def _ssd_finish(y_ref, z_ref, par_ref, o_ref):
    o_ref[...] = _rms(y_ref[...] * _silu(z_ref[...].astype(F32)), par_ref[3:4, :]).astype(o_ref.dtype)


def _ssd_prompt_body(z_ref, x_ref, b_ref, c_ref, dt_ref, cw_ref, par_ref, tri_ref, seg_ref,
                     o_ref, st_ref, ext_ref, xbc_ref, y_ref, xddt_ref):
    first = pl.program_id(1) == 0

    @pl.when(first)
    def _():
        ext_ref[:, 0:8, :] = jnp.zeros((XBC_TILES, 8, LANES), F32)
        st_ref[...] = jnp.zeros_like(st_ref)

    for t in range(XBC_TILES):
        lo = t * LANES
        if t < B_TILE0:
            ext_ref[t, 8:8 + CHUNK, :] = x_ref[:, lo:lo + LANES]
        elif t < C_TILE0:
            ext_ref[t, 8:8 + CHUNK, :] = b_ref[:, lo - D_INNER:lo - D_INNER + LANES]
        else:
            ext_ref[t, 8:8 + CHUNK, :] = c_ref[:, lo - D_INNER - BC_DIM:lo - D_INNER - BC_DIM + LANES]

    for t in range(XBC_TILES):
        acc = cw_ref[t, CONV_W:CONV_W + 1, :] + ext_ref[t, 5:5 + CHUNK, :] * cw_ref[t, 0:1, :]
        for k in range(1, CONV_W):
            acc = acc + ext_ref[t, 5 + k:5 + k + CHUNK, :] * cw_ref[t, k:k + 1, :]
        xbc_ref[t] = _silu(acc)
    ext_ref[:, 0:8, :] = ext_ref[:, CHUNK:CHUNK + 8, :]

    e_acum, e_end = _ssd_intra(xbc_ref, dt_ref[...], par_ref, tri_ref, seg_ref, y_ref, xddt_ref)

    lane = lax.broadcasted_iota(jnp.int32, (CHUNK, LANES), 1)
    gw = SSD_STATE * 2
    for grp in range(SSD_GROUPS):
        rows = slice(grp * gw, (grp + 1) * gw)
        prev = st_ref[rows, :]
        y_off = _dot_nt(xbc_ref[C_TILE0 + grp].astype(BF16), prev.astype(BF16))
        for half in range(2):
            pair = 2 * grp + half
            plo = pair * LANES
            y_ref[:, plo:plo + LANES] += y_off[:, half * LANES:(half + 1) * LANES] * _pair_cols(e_acum, pair, lane)
        new = _dot(xddt_ref[rows, :].astype(BF16), xbc_ref[B_TILE0 + grp].astype(BF16))
        for k in range(4):
            h = 4 * grp + k
            hr = slice(grp * gw + k * SSD_HEADDIM, grp * gw + (k + 1) * SSD_HEADDIM)
            dec = jnp.broadcast_to(jnp.broadcast_to(e_end[:, h:h + 1], (CHUNK, SSD_STATE))[0:1, :],
                                   (SSD_HEADDIM, SSD_STATE))
            st_ref[hr, :] = prev[k * SSD_HEADDIM:(k + 1) * SSD_HEADDIM, :] * dec + new[k * SSD_HEADDIM:(k + 1) * SSD_HEADDIM, :]

    _ssd_finish(y_ref, z_ref, par_ref, o_ref)


def _ssd_consts(segment):
    r = jnp.arange(CHUNK)
    same = (r[:, None] // segment) == (r[None, :] // segment)
    return (same & (r[None, :] <= r[:, None])).astype(F32), same.astype(F32)


def _ssd_prompt(u_zq, u_xd, conv_wb, par, b, s):
    assert s % CHUNK == 0
    tri, seg = _ssd_consts(CHUNK)
    zq3 = u_zq.reshape(b, s, U_ZQ)
    xd3 = u_xd.reshape(b, s, U_XD)
    col = lambda width, blk: pl.BlockSpec((None, CHUNK, width), lambda i, c: (i, c, blk))
    y, st = pl.pallas_call(
        _ssd_prompt_body,
        grid=(b, s // CHUNK),
        in_specs=[col(D_INNER, 0), col(D_INNER, 0), col(BC_DIM, D_INNER // BC_DIM),
                  col(BC_DIM, D_INNER // BC_DIM + 1), col(LANES, CONV_DIM // LANES),
                  _const_spec((XBC_TILES, 8, LANES)), _const_spec((4, D_INNER)),
                  _const_spec((CHUNK, CHUNK)), _const_spec((CHUNK, CHUNK))],
        out_specs=[pl.BlockSpec((None, CHUNK, D_INNER), lambda i, c: (i, c, 0)),
                   pl.BlockSpec((None, D_INNER, SSD_STATE), lambda i, c: (i, 0, 0))],
        out_shape=[jax.ShapeDtypeStruct((b, s, D_INNER), BF16),
                   jax.ShapeDtypeStruct((b, D_INNER, SSD_STATE), F32)],
        scratch_shapes=[pltpu.VMEM((XBC_TILES, CHUNK + 8, LANES), F32), pltpu.VMEM((XBC_TILES, CHUNK, LANES), F32),
                        pltpu.VMEM((CHUNK, D_INNER), F32), pltpu.VMEM((D_INNER, CHUNK), F32)],
        compiler_params=_params("parallel", "arbitrary"),
        name="ssd_prompt",
    )(zq3, xd3, xd3, xd3, xd3, conv_wb, par, tri, seg)
    return y.reshape(b * s, D_INNER), st


def _conv_sample_body(seq_ref, cw_ref, o_ref, *, tlen):
    acc = cw_ref[4:5, :] + seq_ref[:, 0:tlen, :] * cw_ref[0:1, :]
    for k in range(1, CONV_W):
        acc = acc + seq_ref[:, k:k + tlen, :] * cw_ref[k:k + 1, :]
    o_ref[...] = _silu(acc)


def _conv_sample(seq, conv_wb, tlen):
    b = seq.shape[0]
    bt = 8
    return pl.pallas_call(
        functools.partial(_conv_sample_body, tlen=tlen),
        grid=(b // bt,),
        in_specs=[pl.BlockSpec((bt, CONV_W - 1 + tlen, CONV_DIM), lambda i: (i, 0, 0)),
                  _const_spec((8, CONV_DIM))],
        out_specs=pl.BlockSpec((bt, tlen, CONV_DIM), lambda i: (i, 0, 0)),
        out_shape=jax.ShapeDtypeStruct((b, tlen, CONV_DIM), F32),
        compiler_params=_params("parallel"),
        name="conv_sample",
    )(seq, conv_wb)


SSD_BT = 8


def _ssd_sample_body(z_ref, xbc_in_ref, dt_ref, par_ref, tri_ref, seg_ref, st_in_ref,
                     o_ref, st_out_ref, xbc_ref, dtp_ref, zp_ref, y_ref, xddt_ref, op_ref, *, tlen):
    rows = SSD_BT * tlen
    xbc_ref[...] = jnp.zeros_like(xbc_ref)
    dtp_ref[...] = jnp.zeros_like(dtp_ref)
    zp_ref[...] = jnp.zeros_like(zp_ref)
    for t in range(XBC_TILES):
        xbc_ref[t, 0:rows, :] = xbc_in_ref[:, t * LANES:(t + 1) * LANES]
    dtp_ref[0:rows, :] = dt_ref[...]
    zp_ref[0:rows, :] = z_ref[...].astype(F32)
    e_acum, e_end = _ssd_intra(xbc_ref, dtp_ref[...], par_ref, tri_ref, seg_ref, y_ref, xddt_ref)

    lane = lax.broadcasted_iota(jnp.int32, (CHUNK, LANES), 1)
    row_b = lax.broadcasted_iota(jnp.int32, (CHUNK, 1), 0) // tlen
    win = 16
    per_win = win // tlen
    win_b = lax.broadcasted_iota(jnp.int32, (win, 1), 0) // tlen
    gw = SSD_STATE * 2
    e_end_b = [jnp.broadcast_to(e_end[:, h:h + 1], (CHUNK, SSD_STATE)) for h in range(SSD_HEADS)]
    for grp in range(SSD_GROUPS):
        rsl = slice(grp * gw, (grp + 1) * gw)
        c16 = xbc_ref[C_TILE0 + grp].astype(BF16)
        b32 = xbc_ref[B_TILE0 + grp]
        xddt16 = xddt_ref[rsl, :].astype(BF16)
        for w in range(rows // win):
            c_win = c16[w * win:(w + 1) * win, :]
            y_off = jnp.zeros((win, gw), F32)
            for k in range(per_win):
                bb = w * per_win + k
                prev = st_in_ref[bb, rsl, :]
                y_off = jnp.where(win_b == k, _dot_nt(c_win, prev.astype(BF16)), y_off)
                new = _dot(xddt16, jnp.where(row_b == bb, b32, 0.0).astype(BF16))
                for hh in range(4):
                    h = 4 * grp + hh
                    hs = slice(hh * SSD_HEADDIM, (hh + 1) * SSD_HEADDIM)
                    dec = jnp.broadcast_to(e_end_b[h][bb * tlen:bb * tlen + 1, :], (SSD_HEADDIM, SSD_STATE))
                    st_out_ref[bb, grp * gw + hh * SSD_HEADDIM:grp * gw + (hh + 1) * SSD_HEADDIM, :] = (
                        prev[hs, :] * dec + new[hs, :])
            for half in range(2):
                pair = 2 * grp + half
                plo = pair * LANES
                scale = _pair_cols(e_acum, pair, lane)[w * win:(w + 1) * win, :]
                y_ref[w * win:(w + 1) * win, plo:plo + LANES] += y_off[:, half * LANES:(half + 1) * LANES] * scale

    _ssd_finish(y_ref, zp_ref, par_ref, op_ref)
    o_ref[...] = op_ref[0:rows, :].astype(o_ref.dtype)


def _ssd_sample(u_zq, xbc, u_xd, par, state, layer, b, tlen):
    rows = SSD_BT * tlen
    assert CHUNK % rows == 0 and 16 % tlen == 0 and rows % 16 == 0
    tri, seg = _ssd_consts(tlen)
    row = lambda width, blk: pl.BlockSpec((rows, width), lambda i: (i, blk))
    st_spec = pl.BlockSpec((SSD_BT, D_INNER, SSD_STATE), lambda i: (i, 0, 0))
    st_in_spec = pl.BlockSpec((None, SSD_BT, D_INNER, SSD_STATE), lambda i: (layer, i, 0, 0))
    y, st = pl.pallas_call(
        functools.partial(_ssd_sample_body, tlen=tlen),
        grid=(b // SSD_BT,),
        in_specs=[row(D_INNER, 0), row(CONV_DIM, 0), row(LANES, CONV_DIM // LANES), _const_spec((4, D_INNER)),
                  _const_spec((CHUNK, CHUNK)), _const_spec((CHUNK, CHUNK)), st_in_spec],
        out_specs=[row(D_INNER, 0), st_spec],
        out_shape=[jax.ShapeDtypeStruct((b * tlen, D_INNER), BF16),
                   jax.ShapeDtypeStruct((b, D_INNER, SSD_STATE), F32)],
        scratch_shapes=[pltpu.VMEM((XBC_TILES, CHUNK, LANES), F32), pltpu.VMEM((CHUNK, LANES), F32),
                        pltpu.VMEM((CHUNK, D_INNER), F32), pltpu.VMEM((CHUNK, D_INNER), F32),
                        pltpu.VMEM((D_INNER, CHUNK), F32), pltpu.VMEM((CHUNK, D_INNER), F32)],
        compiler_params=_params("parallel"),
        name="ssd_sample",
    )(u_zq, xbc, u_xd, par, tri, seg, state)
    return y, st


def _prep_weights(norm_g, w_ffn_gu, w_ffn_down, w_in_a, conv_w, conv_b, dt_bias, a_log, d_skip, ssd_norm_g,
                  w_out_a, w_in_b, w_out_b, kv_norm_g, w_kv_shared, mem_norm_g, w_mem_kv, final_norm_g):
    p = {}
    p["g_ffn"] = [[jnp.stack([norm_g[l, 0 + 4 * k], norm_g[l, 1 + 4 * k], final_norm_g]) for k in range(2)]
                  for l in range(DEPTH)]
    p["w_gu"] = w_ffn_gu.astype(BF16)
    p["w_down"] = w_ffn_down.astype(BF16)
    p["w_zq"] = jnp.concatenate([w_in_a[:, :, :D_INNER], w_in_a[:, :, SSD_IN:]], axis=-1).astype(BF16)
    w_dt = jnp.pad(w_in_a[:, :, SSD_MAIN:SSD_IN], ((0, 0), (0, 0), (0, LANES - SSD_HEADS)))
    p["w_xd"] = jnp.concatenate([w_in_a[:, :, D_INNER:SSD_MAIN], w_dt], axis=-1).astype(BF16)
    p["g_in_a"] = norm_g[:N_A, 2]
    p["g_in_b"] = norm_g[N_A:, 2]
    p["conv_wb"] = jnp.concatenate([conv_w, conv_b[:, None, :], jnp.zeros((N_A, 3, CONV_DIM), F32)], axis=1)
    p["conv_wb_tiles"] = p["conv_wb"].reshape(N_A, 8, XBC_TILES, LANES).transpose(0, 2, 1, 3)
    lane_pad = lambda v: jnp.pad(v, ((0, 0), (0, D_INNER - v.shape[1])))
    p["ssd_par"] = jnp.stack([lane_pad(dt_bias), lane_pad(a_log), jnp.repeat(d_skip, SSD_HEADDIM, axis=1),
                              ssd_norm_g], axis=1)
    p["w_out_a"] = w_out_a.astype(BF16)
    q_dil = N_DIL * DIL_GW
    p["w_in_b"] = jnp.concatenate([w_in_b[:, :, q_dil:], w_in_b[:, :, :q_dil]], axis=-1).astype(BF16)
    p["w_out_b"] = w_out_b.astype(BF16)
    p["w_kv"] = w_kv_shared.astype(BF16)
    p["w_mem_kv"] = w_mem_kv.astype(BF16)
    n = N_DIL * DIL_QH
    p["slopes"] = jnp.exp2(-8.0 * jnp.arange(1, n + 1, dtype=F32) / n)
    p["norm_g"] = norm_g
    p["kv_norm_g"] = kv_norm_g
    p["mem_norm_g"] = mem_norm_g
    return p


def _trunk(p, x, b, s, mem_kv, conv_prev, ssm_prev, win_bufs):
    prompt = win_bufs is None
    ssm_new, conv_new = [], []
    kv = None
    for l in range(DEPTH):
        g = p["norm_g"][l]
        x = _ffn(x, p["g_ffn"][l][0], p["w_gu"], p["w_down"], l, 0)
        if l < N_A:
            u_zq = _norm_matmul(x, p["g_in_a"], p["w_zq"], layers=(l,), out_dtype=BF16)[0]
            u_xd = _norm_matmul(x, p["g_in_a"], p["w_xd"], layers=(l,))[0]
            raw = u_xd.reshape(b, s, U_XD)
            if prompt:
                y_mix, st = _ssd_prompt(u_zq, u_xd, p["conv_wb_tiles"][l], p["ssd_par"][l], b, s)
                conv_new.append(raw[:, s - (CONV_W - 1):, :CONV_DIM])
                mem_o = _mem_attn_prompt(u_zq, D_INNER // MEM_Q, mem_kv, l, b, s)
            else:
                seq = jnp.concatenate([conv_prev[l], raw[:, :, :CONV_DIM]], axis=1)
                xbc = _conv_sample(seq, p["conv_wb"][l], s).reshape(b * s, CONV_DIM)
                y_mix, st = _ssd_sample(u_zq, xbc, u_xd, p["ssd_par"][l], ssm_prev, l, b, s)
                conv_new.append(seq[:, s:])
                mem_o = _mem_attn_sample(u_zq, D_INNER // MEM_Q, mem_kv, l, b, s)
            ssm_new.append(st.reshape(b, SSD_HEADS, SSD_HEADDIM, SSD_STATE))
            x = _out_proj_a(y_mix, mem_o, x, g[3], p["w_out_a"], l)
        else:
            i = l - N_A
            u = _norm_matmul(x, p["g_in_b"], p["w_in_b"], layers=(i,))[0]
            outs, lses = [], []
            for gi in range(N_DIL):
                if prompt:
                    o, lse = _dil_attn_prompt(p["slopes"], u, kv, gi, b, s)
                else:
                    o, lse = _dil_attn_sample(p["slopes"], u, kv, win_bufs[gi], gi, b, s)
                outs.append(o)
                lses.append(lse)
            if prompt:
                mem_o = _mem_attn_prompt(u, 0, mem_kv, l, b, s)
            else:
                mem_o = _mem_attn_sample(u, 0, mem_kv, l, b, s)
            x = _out_proj_b(outs, lses, mem_o, x, g[3], p["w_out_b"], i)
        x = _ffn(x, p["g_ffn"][l][1], p["w_gu"], p["w_down"], l, 1, final=(l == DEPTH - 1))
        if l == N_A - 1:
            kv = _norm_matmul(x, p["kv_norm_g"][None], p["w_kv"][None])[0]
    return x, jnp.stack(conv_new), jnp.stack(ssm_new), kv


def kernel(x_prompt, x_sample, mem_prompt, cache_mem_kv, state_ssm, state_conv, cache_win_g1, cache_win_g2,
           cache_win_g3, norm_g, w_ffn_gu, w_ffn_down, w_in_a, conv_w, conv_b, dt_bias, a_log, d_skip,
           ssd_norm_g, w_out_a, w_in_b, w_out_b, kv_norm_g, w_kv_shared, mem_norm_g, w_mem_kv, final_norm_g):
    p = _prep_weights(norm_g, w_ffn_gu, w_ffn_down, w_in_a, conv_w, conv_b, dt_bias, a_log, d_skip, ssd_norm_g,
                      w_out_a, w_in_b, w_out_b, kv_norm_g, w_kv_shared, mem_norm_g, w_mem_kv, final_norm_g)
    bp, sp = x_prompt.shape[:2]
    bs, ss = x_sample.shape[:2]

    mem_flat = mem_prompt.reshape(bp * N_MEM, D_MODEL)
    mem_kv_p = _norm_matmul(mem_flat, mem_norm_g, p["w_mem_kv"]).reshape(DEPTH, bp, N_MEM, 2 * MEM_Q)
    y_p, conv_p, ssm_p, kv_p = _trunk(p, x_prompt.reshape(bp * sp, D_MODEL), bp, sp, mem_kv_p, None, None, None)
    kv_p = kv_p.reshape(bp, sp, KV_W)
    win_p = [kv_p[:, sp - min(w, sp):, gi * DIL_GW:(gi + 1) * DIL_GW].reshape(bp, min(w, sp), 2, DIL_KVH, DIL_HD)
             for gi, (w, _) in enumerate(DIL_CONFIGS)]

    y_s, conv_s, ssm_s, kv_s = _trunk(p, x_sample.reshape(bs * ss, D_MODEL), bs, ss, cache_mem_kv, state_conv,
                                      state_ssm.reshape(N_A, bs, D_INNER, SSD_STATE),
                                      (cache_win_g1, cache_win_g2, cache_win_g3))
    kv_s = kv_s.reshape(bs, ss, N_DIL, 2, DIL_KVH, DIL_HD)
    return (y_p.reshape(bp, sp, D_MODEL), y_s.reshape(bs, ss, D_MODEL),
            mem_kv_p.reshape(DEPTH, bp, N_MEM, 2, MEM_HEADS, MEM_HD), ssm_p, conv_p, win_p[0], win_p[1], win_p[2],
            ssm_s, conv_s, kv_s[:, :, 0], kv_s[:, :, 1], kv_s[:, :, 2])
```
